```python
import math
import jax, jax.numpy as jnp
from jax import lax
import numpy as np

D_MODEL = 1024
BATCH = 16
SEQ = 4096
DEPTH = 4
DEC_BATCH = 32
DEC_SEQ = 64
PAST_LEN = 4096

CHUNK = 64
Q_BLOCK = 128
EPS = 1e-6
GLA_HEADS = 4
GLA_DK = 32
GLA_DV = 64
GLA_GATE_RANK = 16
GLA_TAU = 16.0
POOL_WINDOWS = (2, 4, 8, 16)
POOL_GROUP = 64
POOL_WIDTH = 256
POOL_HIST = 15
DIFF_HEADS = 4
DIFF_DH = 64
DIFF_DV = 128
MEM_LEN = 256
MEM_HEADS = 4
MEM_DH = 128
N_GROUPS = 4
EXPERTS_PER_GROUP = 4
N_EXPERTS = 16
TOP_K = 2
EXPERT_HIDDEN = 512

GLA_QK = GLA_HEADS * GLA_DK
GLA_V = GLA_HEADS * GLA_DV
DIFF_QK = DIFF_HEADS * 2 * DIFF_DH
DIFF_V = DIFF_HEADS * DIFF_DV
MIX_WIDTH = GLA_V + POOL_WIDTH + DIFF_V
IN_SIZES = (GLA_QK, GLA_QK, GLA_V, GLA_V, GLA_GATE_RANK, POOL_WIDTH, DIFF_QK, DIFF_QK, DIFF_V)
IN_WIDTH = 2 * GLA_QK + 2 * GLA_V + GLA_GATE_RANK + POOL_WIDTH + 2 * DIFF_QK + DIFF_V

kernel_name = 'hybrid_stream_encoder_step'


def rmsnorm(x, g):
    xf = x.astype(jnp.float32)
    y = xf * lax.rsqrt(jnp.mean(xf * xf, axis=-1, keepdims=True) + EPS)
    return (y * g.astype(jnp.float32)).astype(x.dtype)


def gla_recurrence(q, k, v, log_a, s0):
    B, L, H, DK = q.shape
    DV = v.shape[-1]
    C = min(CHUNK, L)
    n = L // C
    causal = jnp.tril(jnp.ones((C, C), dtype=bool))

    def to_chunks(t):
        return t.reshape(B, n, C, H, t.shape[-1]).swapaxes(0, 1)

    def step(s, inp):
        qc, kc, vc, ac = inp
        b = jnp.cumsum(ac, axis=1)
        rel = b[:, :, None] - b[:, None, :]
        decay = jnp.exp(jnp.where(causal[None, :, :, None, None], rel, -jnp.inf))
        att = jnp.einsum('bthk,btshk,bshk->bhts', qc, decay, kc)
        o = (jnp.einsum('bhts,bshv->bthv', att, vc)
             + jnp.einsum('bthk,bhkv->bthv', qc * jnp.exp(b), s))
        b_end = b[:, -1]
        s_new = (jnp.exp(b_end)[..., None] * s
                 + jnp.einsum('bshk,bshv->bhkv', kc * jnp.exp(b_end[:, None] - b), vc))
        return s_new, o

    s_fin, o = lax.scan(step, s0, (to_chunks(q), to_chunks(k), to_chunks(v), to_chunks(log_a)))
    return o.swapaxes(0, 1).reshape(B, L, H, DV), s_fin


def pool_mixer(u, hist, pos0, w_pool, scale):
    B, L, _ = u.shape
    ext = jnp.concatenate([hist.astype(u.dtype), u], axis=1)
    cs = jnp.cumsum(ext.astype(jnp.float32), axis=1)
    cs = jnp.concatenate([jnp.zeros((B, 1, POOL_WIDTH), jnp.float32), cs], axis=1)
    pos = pos0 + jnp.arange(L)
    outs = []
    for gi, w in enumerate(POOL_WINDOWS):
        lo, hi = gi * POOL_GROUP, (gi + 1) * POOL_GROUP
        win_sum = (cs[:, POOL_HIST + 1:POOL_HIST + 1 + L, lo:hi]
                   - cs[:, POOL_HIST + 1 - w:POOL_HIST + 1 - w + L, lo:hi])
        cnt = jnp.minimum(w, pos + 1).astype(jnp.float32)[None, :, None]
        m = win_sum / cnt - u[..., lo:hi].astype(jnp.float32)
        outs.append(m @ w_pool[gi].astype(jnp.float32))
    out = jnp.concatenate(outs, axis=-1) * scale.astype(jnp.float32)
    return out, ext[:, -POOL_HIST:]


def diff_attention(q, k, v, q_pos, k_pos, lam):
    B, Lq, H, _, DH = q.shape
    DV = v.shape[-1]
    scale = DH ** -0.5
    k_chunk = k_pos // CHUNK

    def attend(qb, qpb):
        s = jnp.einsum('bqhmd,bkhmd->bhmqk', qb, k).astype(jnp.float32) * scale
        mask = k_chunk[None, :] <= (qpb // CHUNK)[:, None]
        s = jnp.where(mask[None, None, None], s, -jnp.inf)
        p = jax.nn.softmax(s, axis=-1)
        pd = p[:, :, 0] - lam * p[:, :, 1]
        return jnp.einsum('bhqk,bkhv->bqhv', pd.astype(v.dtype), v)

    if Lq > Q_BLOCK:
        nb = Lq // Q_BLOCK
        qb = q.reshape(B, nb, Q_BLOCK, H, 2, DH).swapaxes(0, 1)
        ob = lax.map(lambda a: attend(a[0], a[1]), (qb, q_pos.reshape(nb, Q_BLOCK)))
        return ob.swapaxes(0, 1).reshape(B, Lq, H, DV)
    return attend(q, q_pos)


def memory_kv(mem, g, w_kv):
    B, M, _ = mem.shape
    kv = rmsnorm(mem, g) @ w_kv
    mk, mv = jnp.split(kv, 2, axis=-1)
    return mk.reshape(B, M, MEM_HEADS, MEM_DH), mv.reshape(B, M, MEM_HEADS, MEM_DH)


def memory_attend(h, mk, mv, w_q, w_o):
    B, L, _ = h.shape
    q = (h @ w_q).reshape(B, L, MEM_HEADS, MEM_DH)
    s = jnp.einsum('bqhd,bkhd->bhqk', q, mk.astype(q.dtype)).astype(jnp.float32) * MEM_DH ** -0.5
    p = jax.nn.softmax(s, axis=-1)
    o = jnp.einsum('bhqk,bkhd->bqhd', p.astype(mv.dtype), mv).reshape(B, L, MEM_HEADS * MEM_DH)
    return (o.astype(h.dtype) @ w_o).astype(h.dtype)


def hier_moe(h, w_rg, b_rg, w_re, b_re, w_g, w_u, w_d):
    B, L, D = h.shape
    t = h.reshape(B * L, D)
    T = t.shape[0]
    g_prob = jax.nn.softmax((t @ w_rg).astype(jnp.float32) + b_rg.astype(jnp.float32), axis=-1)
    g_top = jnp.argmax(g_prob, axis=-1)
    g_onehot = jax.nn.one_hot(g_top, N_GROUPS, dtype=jnp.float32)
    g_w = jnp.sum(g_prob * g_onehot, axis=-1, keepdims=True)
    e_logit = ((t @ w_re).astype(jnp.float32) + b_re.astype(jnp.float32)).reshape(T, N_GROUPS, EXPERTS_PER_GROUP)
    e_in = jnp.sum(e_logit * g_onehot[:, :, None], axis=1)
    top_v, top_i = lax.top_k(jax.nn.softmax(e_in, axis=-1), TOP_K)
    top_v = top_v / jnp.sum(top_v, axis=-1, keepdims=True)
    within = jnp.sum(jax.nn.one_hot(top_i, EXPERTS_PER_GROUP, dtype=jnp.float32) * top_v[..., None], axis=1)
    gate = (g_onehot[:, :, None] * (g_w * within)[:, None, :]).reshape(T, N_EXPERTS)
    out = jnp.zeros((T, D), jnp.float32)
    for e in range(N_EXPERTS):
        y = (jax.nn.silu(t @ w_g[e]) * (t @ w_u[e])) @ w_d[e]
        out = out + gate[:, e:e + 1] * y.astype(jnp.float32)
    return out.reshape(B, L, D).astype(h.dtype)


def layer(p, lam_init, x, gla_s0, pool_hist, k_past, v_past, mk, mv):
    B, L, _ = x.shape
    P = 0 if k_past is None else k_past.shape[1]
    h = rmsnorm(x, p['norm_mix_g'])
    z = h @ p['w_in']
    idx = np.cumsum(IN_SIZES)[:-1].tolist()
    gq, gk, gv, gr, ga, pu, dq, dk, dv = jnp.split(z, idx, axis=-1)

    q = gq.reshape(B, L, GLA_HEADS, GLA_DK).astype(jnp.float32) * GLA_DK ** -0.5
    k = gk.reshape(B, L, GLA_HEADS, GLA_DK).astype(jnp.float32)
    v = gv.reshape(B, L, GLA_HEADS, GLA_DV).astype(jnp.float32)
    log_a = jax.nn.log_sigmoid((ga @ p['w_gla_a2']).astype(jnp.float32)
                               + p['b_gla_a'].astype(jnp.float32)) / GLA_TAU
    o_gla, gla_s = gla_recurrence(q, k, v, log_a.reshape(B, L, GLA_HEADS, GLA_DK), gla_s0.astype(jnp.float32))
    o_gla = rmsnorm(o_gla, p['gla_norm_g']).reshape(B, L, GLA_V) * jax.nn.silu(gr.astype(jnp.float32))

    o_pool, pool_buf = pool_mixer(pu, pool_hist, P, p['pool_w'], p['pool_scale'])

    dq5 = dq.reshape(B, L, DIFF_HEADS, 2, DIFF_DH)
    k_new = dk.reshape(B, L, DIFF_HEADS, 2 * DIFF_DH)
    v_new = dv.reshape(B, L, DIFF_HEADS, DIFF_DV)
    if k_past is None:
        k_all, v_all = k_new, v_new
    else:
        k_all = jnp.concatenate([k_past.astype(k_new.dtype), k_new], axis=1)
        v_all = jnp.concatenate([v_past.astype(v_new.dtype), v_new], axis=1)
    lam = (jnp.exp(jnp.sum(p['lambda_q1'].astype(jnp.float32) * p['lambda_k1'].astype(jnp.float32)))
           - jnp.exp(jnp.sum(p['lambda_q2'].astype(jnp.float32) * p['lambda_k2'].astype(jnp.float32)))
           + lam_init)
    o_diff = diff_attention(dq5, k_all.reshape(B, P + L, DIFF_HEADS, 2, DIFF_DH), v_all,
                            P + jnp.arange(L), jnp.arange(P + L), lam)
    o_diff = rmsnorm(o_diff, p['diff_norm_g']) * (1.0 - lam_init)

    mix = jnp.concatenate([o_gla.astype(h.dtype), o_pool.astype(h.dtype),
                           o_diff.reshape(B, L, DIFF_V).astype(h.dtype)], axis=-1)
    x = x + (mix @ p['w_out']).astype(x.dtype)
    x = x + memory_attend(rmsnorm(x, p['norm_mem_g']), mk, mv, p['w_mem_q'], p['w_mem_o'])
    x = x + hier_moe(rmsnorm(x, p['norm_ffn_g']), p['w_router_group'], p['b_router_group'],
                     p['w_router_expert'], p['b_router_expert'],
                     p['w_exp_gate'], p['w_exp_up'], p['w_exp_down'])
    return x, gla_s, pool_buf, k_new, v_new


def setup_inputs(seed: int = 0) -> dict:
    key = jax.random.key(seed)
    ks = jax.random.split(key, 40)

    def nrm(i, shape, scale):
        return scale * jax.random.normal(ks[i], shape, jnp.float32)

    def gain(i, shape):
        return 1.0 + nrm(i, shape, 0.02)

    return {
        'x_prompt': nrm(0, (BATCH, SEQ, D_MODEL), 1.0),
        'x_sample': nrm(1, (DEC_BATCH, DEC_SEQ, D_MODEL), 1.0),
        'mem_prompt': nrm(2, (BATCH, MEM_LEN, D_MODEL), 1.0),
        'cache_diff_k': nrm(3, (DEPTH, DEC_BATCH, PAST_LEN, DIFF_HEADS, 2 * DIFF_DH), 1.0),
        'cache_diff_v': nrm(4, (DEPTH, DEC_BATCH, PAST_LEN, DIFF_HEADS, DIFF_DV), 1.0),
        'cache_mem_k': nrm(5, (DEPTH, DEC_BATCH, MEM_LEN, MEM_HEADS, MEM_DH), 1.0),
        'cache_mem_v': nrm(6, (DEPTH, DEC_BATCH, MEM_LEN, MEM_HEADS, MEM_DH), 1.0),
        'state_gla': nrm(7, (DEPTH, DEC_BATCH, GLA_HEADS, GLA_DK, GLA_DV), 0.5),
        'state_pool': nrm(8, (DEPTH, DEC_BATCH, POOL_HIST, POOL_WIDTH), 1.0),
        'norm_mix_g': gain(9, (DEPTH, D_MODEL)),
        'w_in': nrm(10, (DEPTH, D_MODEL, IN_WIDTH), D_MODEL ** -0.5),
        'w_gla_a2': nrm(11, (DEPTH, GLA_GATE_RANK, GLA_QK), GLA_GATE_RANK ** -0.5),
        'b_gla_a': nrm(12, (DEPTH, GLA_QK), 0.1),
        'gla_norm_g': gain(13, (DEPTH, GLA_DV)),
        'pool_w': nrm(14, (DEPTH, len(POOL_WINDOWS), POOL_GROUP, POOL_GROUP), POOL_GROUP ** -0.5),
        'pool_scale': gain(15, (DEPTH, POOL_WIDTH)),
        'lambda_q1': nrm(16, (DEPTH, DIFF_DH), 0.1),
        'lambda_k1': nrm(17, (DEPTH, DIFF_DH), 0.1),
        'lambda_q2': nrm(18, (DEPTH, DIFF_DH), 0.1),
        'lambda_k2': nrm(19, (DEPTH, DIFF_DH), 0.1),
        'diff_norm_g': gain(20, (DEPTH, DIFF_DV)),
        'w_out': nrm(21, (DEPTH, MIX_WIDTH, D_MODEL), MIX_WIDTH ** -0.5),
        'norm_mem_g': gain(22, (DEPTH, D_MODEL)),
        'mem_norm_g': gain(23, (DEPTH, D_MODEL)),
        'w_mem_q': nrm(24, (DEPTH, D_MODEL, MEM_HEADS * MEM_DH), D_MODEL ** -0.5),
        'w_mem_kv': nrm(25, (DEPTH, D_MODEL, 2 * MEM_HEADS * MEM_DH), D_MODEL ** -0.5),
        'w_mem_o': nrm(26, (DEPTH, MEM_HEADS * MEM_DH, D_MODEL), (MEM_HEADS * MEM_DH) ** -0.5),
        'norm_ffn_g': gain(27, (DEPTH, D_MODEL)),
        'w_router_group': nrm(28, (DEPTH, D_MODEL, N_GROUPS), D_MODEL ** -0.5),
        'b_router_group': nrm(29, (DEPTH, N_GROUPS), 0.01),
        'w_router_expert': nrm(30, (DEPTH, D_MODEL, N_EXPERTS), D_MODEL ** -0.5),
        'b_router_expert': nrm(31, (DEPTH, N_EXPERTS), 0.01),
        'w_exp_gate': nrm(32, (DEPTH, N_EXPERTS, D_MODEL, EXPERT_HIDDEN), D_MODEL ** -0.5),
        'w_exp_up': nrm(33, (DEPTH, N_EXPERTS, D_MODEL, EXPERT_HIDDEN), D_MODEL ** -0.5),
        'w_exp_down': nrm(34, (DEPTH, N_EXPERTS, EXPERT_HIDDEN, D_MODEL), EXPERT_HIDDEN ** -0.5),
        'final_norm_g': gain(35, (D_MODEL,)),
    }


def reference(x_prompt, x_sample, mem_prompt, cache_diff_k, cache_diff_v, cache_mem_k, cache_mem_v,
              state_gla, state_pool, norm_mix_g, w_in, w_gla_a2, b_gla_a, gla_norm_g, pool_w, pool_scale,
              lambda_q1, lambda_k1, lambda_q2, lambda_k2, diff_norm_g, w_out, norm_mem_g, mem_norm_g,
              w_mem_q, w_mem_kv, w_mem_o, norm_ffn_g, w_router_group, b_router_group,
              w_router_expert, b_router_expert, w_exp_gate, w_exp_up, w_exp_down, final_norm_g):
    xp, xs = x_prompt, x_sample
    bp = x_prompt.shape[0]
    dk_p, dv_p, mk_p, mv_p, gs_p, ps_p = [], [], [], [], [], []
    dk_s, dv_s, gs_s, ps_s = [], [], [], []
    for l in range(DEPTH):
        p = dict(norm_mix_g=norm_mix_g[l], w_in=w_in[l], w_gla_a2=w_gla_a2[l], b_gla_a=b_gla_a[l],
                 gla_norm_g=gla_norm_g[l], pool_w=pool_w[l], pool_scale=pool_scale[l],
                 lambda_q1=lambda_q1[l], lambda_k1=lambda_k1[l], lambda_q2=lambda_q2[l],
                 lambda_k2=lambda_k2[l], diff_norm_g=diff_norm_g[l], w_out=w_out[l],
                 norm_mem_g=norm_mem_g[l], w_mem_q=w_mem_q[l], w_mem_o=w_mem_o[l],
                 norm_ffn_g=norm_ffn_g[l], w_router_group=w_router_group[l],
                 b_router_group=b_router_group[l], w_router_expert=w_router_expert[l],
                 b_router_expert=b_router_expert[l], w_exp_gate=w_exp_gate[l],
                 w_exp_up=w_exp_up[l], w_exp_down=w_exp_down[l])
        lam_init = 0.8 - 0.6 * math.exp(-0.3 * l)
        mk, mv = memory_kv(mem_prompt, mem_norm_g[l], w_mem_kv[l])
        xp, g_new, pool_new, k_new, v_new = layer(
            p, lam_init, xp, jnp.zeros((bp, GLA_HEADS, GLA_DK, GLA_DV), jnp.float32),
            jnp.zeros((bp, POOL_HIST, POOL_WIDTH), xp.dtype), None, None, mk, mv)
        dk_p.append(k_new); dv_p.append(v_new); mk_p.append(mk); mv_p.append(mv)
        gs_p.append(g_new); ps_p.append(pool_new)
        xs, g_new, pool_new, k_new, v_new = layer(
            p, lam_init, xs, state_gla[l], state_pool[l], cache_diff_k[l], cache_diff_v[l],
            cache_mem_k[l], cache_mem_v[l])
        dk_s.append(k_new); dv_s.append(v_new); gs_s.append(g_new); ps_s.append(pool_new)
    y_prompt = rmsnorm(xp, final_norm_g)
    y_sample = rmsnorm(xs, final_norm_g)
    return (y_prompt, y_sample,
            jnp.stack(dk_p), jnp.stack(dv_p), jnp.stack(mk_p), jnp.stack(mv_p),
            jnp.stack(gs_p), jnp.stack(ps_p),
            jnp.stack(dk_s), jnp.stack(dv_s), jnp.stack(gs_s), jnp.stack(ps_s))
```

```python
import functools
import math

import jax
import jax.numpy as jnp
from jax import lax
from jax.experimental import pallas as pl
from jax.experimental.pallas import tpu as pltpu

F32 = jnp.float32
BF16 = jnp.bfloat16

EPS = 1e-6
CHUNK = 64
GLA_HEADS, GLA_DK, GLA_DV = 4, 32, 64
GLA_GATE_RANK = 16
GLA_TAU = 16.0
GLA_QK = GLA_HEADS * GLA_DK
GLA_V = GLA_HEADS * GLA_DV
POOL_WINDOWS = (2, 4, 8, 16)
POOL_GROUP = 64
POOL_WIDTH = 256
POOL_HIST = 15
POOL_CARRY = 16
DIFF_HEADS, DIFF_DH, DIFF_DV = 4, 64, 128
DIFF_QK = DIFF_HEADS * 2 * DIFF_DH
DIFF_V = DIFF_HEADS * DIFF_DV
MEM_HEADS, MEM_DH = 4, 128
N_GROUPS, EXPERTS_PER_GROUP, N_EXPERTS = 4, 4, 16
LANES = 128
GLA_PACK = 2 * GLA_QK + 2 * GLA_V + LANES
ROUTER_E0 = N_GROUPS
VMEM_LIMIT = 56 * 1024 * 1024


def _cparams(sem):
    return pltpu.CompilerParams(dimension_semantics=sem, vmem_limit_bytes=VMEM_LIMIT)


def _rms(x, g):
    ms = jnp.mean(x * x, axis=-1, keepdims=True)
    return x * lax.rsqrt(ms + EPS) * g


def _dot(a, b):
    return jnp.dot(a, b, preferred_element_type=F32)


def _dot_nt(a, b):
    return lax.dot_general(a, b, (((1,), (1,)), ((), ())), preferred_element_type=F32)


def _dot_tn(a, b):
    return lax.dot_general(a, b, (((0,), (0,)), ((), ())), preferred_element_type=F32)


def _norm_proj_body(x_ref, g_ref, w_ref, *o_refs, groups):
    h = _rms(x_ref[...], g_ref[...]).astype(BF16)
    k = 0
    for a, b, dts in groups:
        z = _dot(h, w_ref[:, a:b])
        for dt in dts:
            o_refs[k][...] = z.astype(dt)
            k += 1


def norm_proj(x, g, w, groups, tm):
    T, D = x.shape
    out_shape, out_specs = [], []
    for a, b, dts in groups:
        for dt in dts:
            out_shape.append(jax.ShapeDtypeStruct((T, b - a), dt))
            out_specs.append(pl.BlockSpec((tm, b - a), lambda i: (i, 0)))
    return pl.pallas_call(
        functools.partial(_norm_proj_body, groups=groups),
        grid=(T // tm,),
        in_specs=[pl.BlockSpec((tm, D), lambda i: (i, 0)),
                  pl.BlockSpec((1, D), lambda i: (0, 0)),
                  pl.BlockSpec(w.shape, lambda i: (0, 0))],
        out_specs=out_specs, out_shape=out_shape,
        compiler_params=_cparams(("parallel",)),
    )(x, g.reshape(1, D), w)


def _gla_body(zg_ref, wa_ref, ba_ref, gn_ref, s0_ref, o_ref, sout_ref, s_scr, *, n_chunks):
    C = CHUNK
    i = pl.program_id(1)

    @pl.when(i == 0)
    def _():
        s_scr[...] = s0_ref[...]

    r_i = lax.broadcasted_iota(jnp.int32, (C, C), 0)
    c_i = lax.broadcasted_iota(jnp.int32, (C, C), 1)
    causal = c_i <= r_i
    tri = causal.astype(F32)
    causal_h = (lax.broadcasted_iota(jnp.int32, (GLA_HEADS * C, C), 1)
                <= lax.broadcasted_iota(jnp.int32, (GLA_HEADS * C, C), 0) % C)
    head_k = lax.broadcasted_iota(jnp.int32, (C, GLA_QK), 1) // GLA_DK
    head_v = lax.broadcasted_iota(jnp.int32, (C, GLA_V), 1) // GLA_DV
    st_mask = (lax.broadcasted_iota(jnp.int32, (GLA_V, GLA_QK), 0) // GLA_DV
               == lax.broadcasted_iota(jnp.int32, (GLA_V, GLA_QK), 1) // GLA_DK)

    st = s_scr[...]
    for c in range(n_chunks):
        rows = slice(c * C, (c + 1) * C)
        q = zg_ref[rows, 0:GLA_QK] * (GLA_DK ** -0.5)
        k = zg_ref[rows, GLA_QK:2 * GLA_QK]
        v = zg_ref[rows, 2 * GLA_QK:2 * GLA_QK + GLA_V].astype(BF16)
        gr = zg_ref[rows, 2 * GLA_QK + GLA_V:2 * GLA_QK + 2 * GLA_V]
        ga = zg_ref[rows, 2 * GLA_QK + 2 * GLA_V:GLA_PACK].astype(BF16)
        x = _dot(ga, wa_ref[...]) + ba_ref[...]
        log_a = (jnp.minimum(x, 0.0) - jnp.log1p(jnp.exp(-jnp.abs(x)))) / GLA_TAU
        b = jnp.dot(tri, log_a, preferred_element_type=F32, precision=lax.Precision.HIGHEST)
        b_mid = b[C // 2 - 1:C // 2, :]
        b_end = b[C - 1:C, :]
        q_in = q * jnp.exp(b - b_mid)
        k_in = (k * jnp.exp(b_mid - b)).astype(BF16)
        q_st = jnp.concatenate([jnp.where(head_k == h, q_in, 0.0) for h in range(GLA_HEADS)], axis=0).astype(BF16)
        att = jnp.where(causal_h, _dot_nt(q_st, k_in), 0.0).astype(BF16)
        r = _dot(att, v)
        o = _dot_nt((q * jnp.exp(b)).astype(BF16), st.astype(BF16))
        for h in range(GLA_HEADS):
            o = o + jnp.where(head_v == h, r[h * C:(h + 1) * C], 0.0)
        k_dec = (k * jnp.exp(b_end - b)).astype(BF16)
        st = st * jnp.exp(b_end) + jnp.where(st_mask, _dot_tn(v, k_dec), 0.0)
        o2 = o * o
        ms = jnp.zeros_like(o)
        for h in range(GLA_HEADS):
            mh = jnp.sum(jnp.where(head_v == h, o2, 0.0), axis=-1, keepdims=True) * (1.0 / GLA_DV)
            ms = jnp.where(head_v == h, mh, ms)
        y = o * lax.rsqrt(ms + EPS) * gn_ref[...]
        y = y * (gr * (1.0 / (1.0 + jnp.exp(-gr))))
        o_ref[rows, :] = y.astype(o_ref.dtype)
    s_scr[...] = st

    @pl.when(i == pl.num_programs(1) - 1)
    def _():
        sout_ref[...] = st


def gla_mixer(zg, w_a2, b_a, norm_g, s0):
    B, L, _ = zg.shape
    tg = min(512, L)
    st0 = jnp.zeros((B, GLA_V, GLA_QK), F32)
    for h in range(GLA_HEADS):
        st0 = st0.at[:, h * GLA_DV:(h + 1) * GLA_DV, h * GLA_DK:(h + 1) * GLA_DK].set(
            jnp.swapaxes(s0[:, h].astype(F32), 1, 2))
    wa = jnp.zeros((LANES, GLA_QK), F32).at[:GLA_GATE_RANK].set(w_a2).astype(BF16)
    o, st = pl.pallas_call(
        functools.partial(_gla_body, n_chunks=tg // CHUNK),
        grid=(B, L // tg),
        in_specs=[pl.BlockSpec((None, tg, GLA_PACK), lambda b, i: (b, i, 0)),
                  pl.BlockSpec((LANES, GLA_QK), lambda b, i: (0, 0)),
                  pl.BlockSpec((1, GLA_QK), lambda b, i: (0, 0)),
                  pl.BlockSpec((1, GLA_V), lambda b, i: (0, 0)),
                  pl.BlockSpec((None, GLA_V, GLA_QK), lambda b, i: (b, 0, 0))],
        out_specs=[pl.BlockSpec((None, tg, GLA_V), lambda b, i: (b, i, 0)),
                   pl.BlockSpec((None, GLA_V, GLA_QK), lambda b, i: (b, 0, 0))],
        out_shape=[jax.ShapeDtypeStruct((B, L, GLA_V), BF16),
                   jax.ShapeDtypeStruct((B, GLA_V, GLA_QK), F32)],
        scratch_shapes=[pltpu.VMEM((GLA_V, GLA_QK), F32)],
        compiler_params=_cparams(("parallel", "arbitrary")),
    )(zg, wa, b_a.reshape(1, GLA_QK), jnp.tile(norm_g, GLA_HEADS).reshape(1, GLA_V), st0)
    s_fin = jnp.stack([jnp.swapaxes(st[:, h * GLA_DV:(h + 1) * GLA_DV, h * GLA_DK:(h + 1) * GLA_DK], 1, 2)
                       for h in range(GLA_HEADS)], axis=1)
    return o, s_fin


def _pool_body(u_ref, hist_ref, w_ref, sc_ref, o_ref, buf_ref, carry, *, pos0, tp):
    i = pl.program_id(1)

    @pl.when(i == 0)
    def _():
        carry[...] = hist_ref[...]

    u = u_ref[...]
    ext = jnp.concatenate([carry[...], u], axis=0)
    s2 = ext + pltpu.roll(ext, 1, 0)
    s4 = s2 + pltpu.roll(s2, 2, 0)
    s8 = s4 + pltpu.roll(s4, 4, 0)
    s16 = s8 + pltpu.roll(s8, 8, 0)
    grp = lax.broadcasted_iota(jnp.int32, (tp, POOL_WIDTH), 1) // POOL_GROUP
    pos = pos0 + i * tp + lax.broadcasted_iota(jnp.int32, (tp, POOL_WIDTH), 0)
    win = s16[POOL_CARRY:]
    width = jnp.full((tp, POOL_WIDTH), POOL_WINDOWS[3], jnp.int32)
    for gi, s in ((2, s8), (1, s4), (0, s2)):
        win = jnp.where(grp == gi, s[POOL_CARRY:], win)
        width = jnp.where(grp == gi, POOL_WINDOWS[gi], width)
    cnt = jnp.minimum(width, pos + 1).astype(F32)
    m = win / cnt - u
    o_ref[...] = (_dot(m.astype(BF16), w_ref[...]) * sc_ref[...]).astype(o_ref.dtype)
    tail = ext[tp:]
    carry[...] = tail

    @pl.when(i == pl.num_programs(1) - 1)
    def _():
        buf_ref[...] = tail


def pool_mixer(u, hist, pos0, w_pool, scale):
    B, L, W = u.shape
    tp = min(512, L)
    hist_p = jnp.pad(hist.astype(F32), ((0, 0), (POOL_CARRY - POOL_HIST, 0), (0, 0)))
    wbd = jnp.zeros((W, W), F32)
    for gi in range(len(POOL_WINDOWS)):
        lo = gi * POOL_GROUP
        wbd = wbd.at[lo:lo + POOL_GROUP, lo:lo + POOL_GROUP].set(w_pool[gi])
    o, buf = pl.pallas_call(
        functools.partial(_pool_body, pos0=pos0, tp=tp),
        grid=(B, L // tp),
        in_specs=[pl.BlockSpec((None, tp, W), lambda b, i: (b, i, 0)),
                  pl.BlockSpec((None, POOL_CARRY, W), lambda b, i: (b, 0, 0)),
                  pl.BlockSpec((W, W), lambda b, i: (0, 0)),
                  pl.BlockSpec((1, W), lambda b, i: (0, 0))],
        out_specs=[pl.BlockSpec((None, tp, W), lambda b, i: (b, i, 0)),
                   pl.BlockSpec((None, POOL_CARRY, W), lambda b, i: (b, 0, 0))],
        out_shape=[jax.ShapeDtypeStruct((B, L, W), BF16),
                   jax.ShapeDtypeStruct((B, POOL_CARRY, W), F32)],
        scratch_shapes=[pltpu.VMEM((POOL_CARRY, W), F32)],
        compiler_params=_cparams(("parallel", "arbitrary")),
    )(u, hist_p, wbd.astype(BF16), scale.reshape(1, W))
    return o, buf[:, POOL_CARRY - POOL_HIST:]


def _lam(lq1, lk1, lq2, lk2, lam_init):
    return (jnp.exp(jnp.sum(lq1[...] * lk1[...], axis=-1, keepdims=True))
            - jnp.exp(jnp.sum(lq2[...] * lk2[...], axis=-1, keepdims=True)) + lam_init)


def _half_masks(n):
    lane = lax.broadcasted_iota(jnp.int32, (n, 2 * DIFF_DH), 1)
    return lane < DIFF_DH, lane >= DIFF_DH


def _flash_update(s, v, m_ref, l_ref, acc_ref, idx):
    tk = s.shape[1]
    m_prev = m_ref[idx]
    m_new = jnp.maximum(m_prev, jnp.max(s, axis=-1, keepdims=True))
    alpha = jnp.exp(m_prev - m_new)
    m_cols = pltpu.repeat(m_new, tk // LANES, axis=1) if tk % LANES == 0 else m_new[:, :tk]
    p = jnp.exp(s - m_cols)
    l_ref[idx] = alpha * l_ref[idx] + jnp.sum(p, axis=-1, keepdims=True)
    acc_ref[idx] = alpha * acc_ref[idx] + _dot(p.astype(BF16), v)
    m_ref[idx] = m_new


def _diff_finish(acc_ref, l_ref, i0, i1, lam, g, post_scale):
    o = acc_ref[i0] / l_ref[i0] - lam * (acc_ref[i1] / l_ref[i1])
    return _rms(o, g) * post_scale


def _diff_prompt_body(q_ref, k_ref, v_ref, lq1, lk1, lq2, lk2, g_ref, o_ref, m_ref, l_ref, acc_ref,
                      *, tq, lam_init):
    qi = pl.program_id(1)
    lam = _lam(lq1, lk1, lq2, lk2, lam_init)
    keep = _half_masks(tq)
    r_c = lax.broadcasted_iota(jnp.int32, (tq, tq), 0) // CHUNK
    c_c = lax.broadcasted_iota(jnp.int32, (tq, tq), 1) // CHUNK
    visible = c_c <= r_c
    for h in range(DIFF_HEADS):
        cols = slice(h * DIFF_DV, (h + 1) * DIFF_DV)
        q_h = q_ref[:, cols].astype(F32) * (DIFF_DH ** -0.5)
        q_m = [jnp.where(keep[mm], q_h, 0.0).astype(BF16) for mm in range(2)]
        m_ref[...] = jnp.full(m_ref.shape, -jnp.inf, F32)
        l_ref[...] = jnp.zeros(l_ref.shape, F32)
        acc_ref[...] = jnp.zeros(acc_ref.shape, F32)

        def block(kb, masked):
            ks = pl.ds(pl.multiple_of(kb * tq, tq), tq)
            k_h = k_ref[ks, cols]
            v_h = v_ref[ks, cols]
            for mm in range(2):
                s = _dot_nt(q_m[mm], k_h)
                if masked:
                    s = jnp.where(visible, s, -jnp.inf)
                _flash_update(s, v_h, m_ref, l_ref, acc_ref, mm)

        def body(kb, carry):
            block(kb, False)
            return carry

        lax.fori_loop(0, qi, body, 0)
        block(qi, True)
        y = _diff_finish(acc_ref, l_ref, 0, 1, lam, g_ref[...], 1.0 - lam_init)
        o_ref[:, cols] = y.astype(o_ref.dtype)


def diff_attention_prompt(q, k, v, lam_vecs, norm_g, lam_init):
    B, L, _ = q.shape
    tq = min(512, L)
    vec = pl.BlockSpec((1, DIFF_DH), lambda b, i: (0, 0))
    return pl.pallas_call(
        functools.partial(_diff_prompt_body, tq=tq, lam_init=lam_init),
        grid=(B, L // tq),
        in_specs=[pl.BlockSpec((None, tq, DIFF_QK), lambda b, i: (b, i, 0)),
                  pl.BlockSpec((None, L, DIFF_QK), lambda b, i: (b, 0, 0)),
                  pl.BlockSpec((None, L, DIFF_V), lambda b, i: (b, 0, 0)),
                  vec, vec, vec, vec,
                  pl.BlockSpec((1, DIFF_DV), lambda b, i: (0, 0))],
        out_specs=pl.BlockSpec((None, tq, DIFF_V), lambda b, i: (b, i, 0)),
        out_shape=jax.ShapeDtypeStruct((B, L, DIFF_V), BF16),
        scratch_shapes=[pltpu.VMEM((2, tq, LANES), F32), pltpu.VMEM((2, tq, LANES), F32),
                        pltpu.VMEM((2, tq, DIFF_DV), F32)],
        compiler_params=_cparams(("parallel", "arbitrary")),
    )(q, k, v, *[t.reshape(1, DIFF_DH) for t in lam_vecs], norm_g.reshape(1, DIFF_DV))


def _diff_sample_body(q_ref, kp_ref, vp_ref, kn_ref, vn_ref, lq1, lk1, lq2, lk2, g_ref, o_ref,
                      m_ref, l_ref, acc_ref, *, lam_init):
    j = pl.program_id(1)
    last = pl.num_programs(1) - 1
    lq = q_ref.shape[0]
    keep = _half_masks(lq)

    @pl.when(j == 0)
    def _():
        m_ref[...] = jnp.full(m_ref.shape, -jnp.inf, F32)
        l_ref[...] = jnp.zeros(l_ref.shape, F32)
        acc_ref[...] = jnp.zeros(acc_ref.shape, F32)

    def step(k_blk, v_blk):
        for h in range(DIFF_HEADS):
            cols = slice(h * DIFF_DV, (h + 1) * DIFF_DV)
            q_h = q_ref[:, cols].astype(F32) * (DIFF_DH ** -0.5)
            k_h = k_blk[:, cols].astype(BF16)
            v_h = v_blk[:, cols].astype(BF16)
            for mm in range(2):
                q_m = jnp.where(keep[mm], q_h, 0.0).astype(BF16)
                _flash_update(_dot_nt(q_m, k_h), v_h, m_ref, l_ref, acc_ref, 2 * h + mm)

    @pl.when(j < last)
    def _():
        step(kp_ref, vp_ref)

    @pl.when(j == last)
    def _():
        step(kn_ref, vn_ref)
        lam = _lam(lq1, lk1, lq2, lk2, lam_init)
        for h in range(DIFF_HEADS):
            y = _diff_finish(acc_ref, l_ref, 2 * h, 2 * h + 1, lam, g_ref[...], 1.0 - lam_init)
            o_ref[:, h * DIFF_DV:(h + 1) * DIFF_DV] = y.astype(o_ref.dtype)


def diff_attention_sample(q, k_past, v_past, k_new, v_new, lam_vecs, norm_g, lam_init):
    B, lq, _ = q.shape
    P = k_past.shape[1]
    tk = min(1024, P)
    n_past = P // tk
    vec = pl.BlockSpec((1, DIFF_DH), lambda b, j: (0, 0))
    past = pl.BlockSpec((None, tk, DIFF_QK), lambda b, j: (b, jnp.minimum(j, n_past - 1), 0))
    new = pl.BlockSpec((None, lq, DIFF_QK), lambda b, j: (b, 0, 0))
    n_state = 2 * DIFF_HEADS
    return pl.pallas_call(
        functools.partial(_diff_sample_body, lam_init=lam_init),
        grid=(B, n_past + 1),
        in_specs=[new, past, past, new, new, vec, vec, vec, vec,
                  pl.BlockSpec((1, DIFF_DV), lambda b, j: (0, 0))],
        out_specs=pl.BlockSpec((None, lq, DIFF_V), lambda b, j: (b, 0, 0)),
        out_shape=jax.ShapeDtypeStruct((B, lq, DIFF_V), BF16),
        scratch_shapes=[pltpu.VMEM((n_state, lq, LANES), F32), pltpu.VMEM((n_state, lq, LANES), F32),
                        pltpu.VMEM((n_state, lq, DIFF_DV), F32)],
        compiler_params=_cparams(("parallel", "arbitrary")),
    )(q, k_past, v_past, k_new, v_new, *[t.reshape(1, DIFF_DH) for t in lam_vecs], norm_g.reshape(1, DIFF_DV))


def _out_proj_body(x_ref, og_ref, op_ref, od_ref, w_ref, o_ref):
    y = _dot(og_ref[...], w_ref[0:GLA_V, :])
    y = y + _dot(op_ref[...], w_ref[GLA_V:GLA_V + POOL_WIDTH, :])
    y = y + _dot(od_ref[...], w_ref[GLA_V + POOL_WIDTH:, :])
    o_ref[...] = x_ref[...] + y


def out_proj(x, og, op, od, w, tm):
    T, D = x.shape
    row = lambda n: pl.BlockSpec((tm, n), lambda i: (i, 0))
    return pl.pallas_call(
        _out_proj_body,
        grid=(T // tm,),
        in_specs=[row(D), row(GLA_V), row(POOL_WIDTH), row(DIFF_V), pl.BlockSpec(w.shape, lambda i: (0, 0))],
        out_specs=row(D), out_shape=jax.ShapeDtypeStruct((T, D), F32),
        compiler_params=_cparams(("parallel",)),
    )(x, og, op, od, w)


def _mem_attn_body(x_ref, g_ref, wq_ref, mk_ref, mv_ref, wo_ref, o_ref):
    x = x_ref[...]
    h = _rms(x, g_ref[...]).astype(BF16)
    q = _dot(h, wq_ref[...]).astype(BF16)
    outs = []
    for hd in range(MEM_HEADS):
        cols = slice(hd * MEM_DH, (hd + 1) * MEM_DH)
        s = _dot_nt(q[:, cols], mk_ref[:, cols].astype(BF16)) * (MEM_DH ** -0.5)
        e = jnp.exp(s - jnp.max(s, axis=-1, keepdims=True))
        p = (e / jnp.sum(e, axis=-1, keepdims=True)).astype(BF16)
        outs.append(_dot(p, mv_ref[:, cols].astype(BF16)).astype(BF16))
    o = jnp.concatenate(outs, axis=-1)
    o_ref[...] = x + _dot(o, wo_ref[...])


def mem_attention(x, g, wq, mk, mv, wo):
    B, L, D = x.shape
    M, HD = mk.shape[1], mk.shape[2]
    tm = min(512, L)
    const = lambda shape: pl.BlockSpec(shape, lambda b, i: (0,) * len(shape))
    return pl.pallas_call(
        _mem_attn_body,
        grid=(B, L // tm),
        in_specs=[pl.BlockSpec((None, tm, D), lambda b, i: (b, i, 0)),
                  const((1, D)), const(wq.shape),
                  pl.BlockSpec((None, M, HD), lambda b, i: (b, 0, 0)),
                  pl.BlockSpec((None, M, HD), lambda b, i: (b, 0, 0)),
                  const(wo.shape)],
        out_specs=pl.BlockSpec((None, tm, D), lambda b, i: (b, i, 0)),
        out_shape=jax.ShapeDtypeStruct((B, L, D), F32),
        compiler_params=_cparams(("parallel", "parallel")),
    )(x, g.reshape(1, D), wq, mk, mv, wo)


def _route(logits):
    lane = lax.broadcasted_iota(jnp.int32, logits.shape, 1)
    big = jnp.int32(LANES)
    neg = -jnp.inf
    is_g = lane < N_GROUPS
    lg = jnp.where(is_g, logits, neg)
    g_max = jnp.max(lg, axis=-1, keepdims=True)
    g_top = jnp.min(jnp.where(lg == g_max, lane, big), axis=-1, keepdims=True)
    g_w = 1.0 / jnp.sum(jnp.exp(lg - g_max), axis=-1, keepdims=True)
    e_id = lane - ROUTER_E0
    in_grp = (e_id >= 0) & (e_id < N_EXPERTS) & (e_id // EXPERTS_PER_GROUP == g_top)
    le = jnp.where(in_grp, logits, neg)
    e_max = jnp.max(le, axis=-1, keepdims=True)
    pe = jnp.exp(le - e_max)
    pe = pe / jnp.sum(pe, axis=-1, keepdims=True)
    v1 = jnp.max(pe, axis=-1, keepdims=True)
    i1 = jnp.min(jnp.where(pe == v1, lane, big), axis=-1, keepdims=True)
    pe2 = jnp.where((lane == i1) | ~in_grp, -1.0, pe)
    v2 = jnp.max(pe2, axis=-1, keepdims=True)
    i2 = jnp.min(jnp.where(pe2 == v2, lane, big), axis=-1, keepdims=True)
    tot = v1 + v2
    return jnp.where(lane == i1, g_w * (v1 / tot), jnp.where(lane == i2, g_w * (v2 / tot), 0.0))


def _moe_dense_body(x_ref, g_ref, wr_ref, br_ref, wg_ref, wu_ref, wd_ref, o_ref, h_scr, gate_scr):
    e = pl.program_id(1)

    @pl.when(e == 0)
    def _():
        x = x_ref[...]
        h = _rms(x, g_ref[...]).astype(BF16)
        h_scr[...] = h
        gate_scr[...] = _route(_dot(h, wr_ref[...]) + br_ref[...])
        o_ref[...] = x

    h = h_scr[...]
    a = _dot(h, wg_ref[...])
    a = a * (1.0 / (1.0 + jnp.exp(-a))) * _dot(h, wu_ref[...])
    y = _dot(a.astype(BF16), wd_ref[...])
    lane = lax.broadcasted_iota(jnp.int32, gate_scr.shape, 1)
    ge = jnp.sum(jnp.where(lane == e + ROUTER_E0, gate_scr[...], 0.0), axis=-1, keepdims=True)
    o_ref[...] += ge * y


def moe_dense(x, g, wr, br, wg, wu, wd, tm):
    T, D = x.shape
    E, _, Hd = wg.shape
    return pl.pallas_call(
        _moe_dense_body,
        grid=(T // tm, E),
        in_specs=[pl.BlockSpec((tm, D), lambda i, e: (i, 0)),
                  pl.BlockSpec((1, D), lambda i, e: (0, 0)),
                  pl.BlockSpec((D, LANES), lambda i, e: (0, 0)),
                  pl.BlockSpec((1, LANES), lambda i, e: (0, 0)),
                  pl.BlockSpec((None, D, Hd), lambda i, e: (e, 0, 0)),
                  pl.BlockSpec((None, D, Hd), lambda i, e: (e, 0, 0)),
                  pl.BlockSpec((None, Hd, D), lambda i, e: (e, 0, 0))],
        out_specs=pl.BlockSpec((tm, D), lambda i, e: (i, 0)),
        out_shape=jax.ShapeDtypeStruct((T, D), F32),
        scratch_shapes=[pltpu.VMEM((tm, D), BF16), pltpu.VMEM((tm, LANES), F32)],
        compiler_params=_cparams(("parallel", "arbitrary")),
    )(x, g.reshape(1, D), wr, br, wg, wu, wd)


def _final_norm_body(x_ref, g_ref, o_ref):
    o_ref[...] = _rms(x_ref[...], g_ref[...])


def final_norm(x, g, tm):
    T, D = x.shape
    return pl.pallas_call(
        _final_norm_body,
        grid=(T // tm,),
        in_specs=[pl.BlockSpec((tm, D), lambda i: (i, 0)), pl.BlockSpec((1, D), lambda i: (0, 0))],
        out_specs=pl.BlockSpec((tm, D), lambda i: (i, 0)),
        out_shape=jax.ShapeDtypeStruct((T, D), F32),
        compiler_params=_cparams(("parallel",)),
    )(x, g.reshape(1, D))


_IN_GROUPS = (
    (0, GLA_PACK, (F32,)),
    (GLA_PACK, GLA_PACK + POOL_WIDTH, (F32,)),
    (GLA_PACK + POOL_WIDTH, GLA_PACK + POOL_WIDTH + DIFF_QK, (BF16,)),
    (GLA_PACK + POOL_WIDTH + DIFF_QK, GLA_PACK + POOL_WIDTH + 2 * DIFF_QK, (F32, BF16)),
    (GLA_PACK + POOL_WIDTH + 2 * DIFF_QK, GLA_PACK + POOL_WIDTH + 2 * DIFF_QK + DIFF_V, (F32, BF16)),
)


def _row_tile(T):
    return min(512, T)


def _layer(p, lam_init, x, gla_s0, pool_hist, k_past, v_past, mk, mv):
    B, L, D = x.shape
    T = B * L
    tm = _row_tile(T)
    P = 0 if k_past is None else k_past.shape[1]
    zg, pu, dq, dk, dk16, dv, dv16 = norm_proj(x.reshape(T, D), p['norm_mix_g'], p['w_in'], _IN_GROUPS, tm)
    o_gla, gla_s = gla_mixer(zg.reshape(B, L, GLA_PACK), p['w_gla_a2'], p['b_gla_a'], p['gla_norm_g'], gla_s0)
    o_pool, pool_buf = pool_mixer(pu.reshape(B, L, POOL_WIDTH), pool_hist, P, p['pool_w'], p['pool_scale'])
    lam_vecs = (p['lambda_q1'], p['lambda_k1'], p['lambda_q2'], p['lambda_k2'])
    q3, k3, v3 = (t.reshape(B, L, DIFF_QK) for t in (dq, dk16, dv16))
    if k_past is None:
        o_diff = diff_attention_prompt(q3, k3, v3, lam_vecs, p['diff_norm_g'], lam_init)
    else:
        o_diff = diff_attention_sample(q3, k_past.reshape(B, P, DIFF_QK), v_past.reshape(B, P, DIFF_V), k3, v3,
                                       lam_vecs, p['diff_norm_g'], lam_init)
    x2 = out_proj(x.reshape(T, D), o_gla.reshape(T, GLA_V), o_pool.reshape(T, POOL_WIDTH),
                  o_diff.reshape(T, DIFF_V), p['w_out'], tm)
    M = mk.shape[1]
    x3 = mem_attention(x2.reshape(B, L, D), p['norm_mem_g'], p['w_mem_q'],
                       mk.reshape(B, M, MEM_HEADS * MEM_DH), mv.reshape(B, M, MEM_HEADS * MEM_DH), p['w_mem_o'])
    x4 = moe_dense(x3.reshape(T, D), p['norm_ffn_g'], p['w_router'], p['b_router'],
                   p['w_exp_gate'], p['w_exp_up'], p['w_exp_down'], tm)
    k_new = dk.reshape(B, L, DIFF_HEADS, 2 * DIFF_DH)
    v_new = dv.reshape(B, L, DIFF_HEADS, DIFF_DV)
    return x4.reshape(B, L, D), gla_s, pool_buf, k_new, v_new


def kernel(x_prompt, x_sample, mem_prompt, cache_diff_k, cache_diff_v, cache_mem_k, cache_mem_v, state_gla, state_pool, norm_mix_g, w_in, w_gla_a2, b_gla_a, gla_norm_g, pool_w, pool_scale, lambda_q1, lambda_k1, lambda_q2, lambda_k2, diff_norm_g, w_out, norm_mem_g, mem_norm_g, w_mem_q, w_mem_kv, w_mem_o, norm_ffn_g, w_router_group, b_router_group, w_router_expert, b_router_expert, w_exp_gate, w_exp_up, w_exp_down, final_norm_g):
    depth = w_in.shape[0]
    D = x_prompt.shape[-1]
    bp, mem_len = mem_prompt.shape[0], mem_prompt.shape[1]
    xp, xs = x_prompt, x_sample
    n_gla_in = 2 * GLA_QK + 2 * GLA_V + GLA_GATE_RANK
    mem_hd = MEM_HEADS * MEM_DH
    outs = [[] for _ in range(10)]
    for l in range(depth):
        wi = w_in[l]
        w_in_p = jnp.concatenate([wi[:, :n_gla_in], jnp.zeros((D, GLA_PACK - n_gla_in), wi.dtype), wi[:, n_gla_in:]],
                                 axis=1).astype(BF16)
        pad = LANES - N_GROUPS - N_EXPERTS
        w_router = jnp.concatenate([w_router_group[l], w_router_expert[l], jnp.zeros((D, pad), F32)], axis=1).astype(BF16)
        b_router = jnp.concatenate([b_router_group[l], b_router_expert[l], jnp.zeros((pad,), F32)]).reshape(1, LANES)
        p = dict(norm_mix_g=norm_mix_g[l], w_in=w_in_p, w_gla_a2=w_gla_a2[l], b_gla_a=b_gla_a[l],
                 gla_norm_g=gla_norm_g[l], pool_w=pool_w[l], pool_scale=pool_scale[l],
                 lambda_q1=lambda_q1[l], lambda_k1=lambda_k1[l], lambda_q2=lambda_q2[l], lambda_k2=lambda_k2[l],
                 diff_norm_g=diff_norm_g[l], w_out=w_out[l].astype(BF16), norm_mem_g=norm_mem_g[l],
                 w_mem_q=w_mem_q[l].astype(BF16), w_mem_o=w_mem_o[l].astype(BF16), norm_ffn_g=norm_ffn_g[l],
                 w_router=w_router, b_router=b_router, w_exp_gate=w_exp_gate[l].astype(BF16),
                 w_exp_up=w_exp_up[l].astype(BF16), w_exp_down=w_exp_down[l].astype(BF16))
        lam_init = 0.8 - 0.6 * math.exp(-0.3 * l)
        mk, mv = norm_proj(mem_prompt.reshape(bp * mem_len, D), mem_norm_g[l], w_mem_kv[l].astype(BF16),
                           ((0, mem_hd, (F32,)), (mem_hd, 2 * mem_hd, (F32,))), _row_tile(bp * mem_len))
        mk = mk.reshape(bp, mem_len, MEM_HEADS, MEM_DH)
        mv = mv.reshape(bp, mem_len, MEM_HEADS, MEM_DH)
        xp, g_new, pool_new, k_new, v_new = _layer(
            p, lam_init, xp, jnp.zeros((bp, GLA_HEADS, GLA_DK, GLA_DV), F32),
            jnp.zeros((bp, POOL_HIST, POOL_WIDTH), F32), None, None, mk, mv)
        for lst, t in zip(outs[:6], (k_new, v_new, mk, mv, g_new, pool_new)):
            lst.append(t)
        xs, g_new, pool_new, k_new, v_new = _layer(
            p, lam_init, xs, state_gla[l], state_pool[l], cache_diff_k[l], cache_diff_v[l],
            cache_mem_k[l], cache_mem_v[l])
        for lst, t in zip(outs[6:], (k_new, v_new, g_new, pool_new)):
            lst.append(t)
    y_prompt = final_norm(xp.reshape(-1, D), final_norm_g, _row_tile(xp.shape[0] * xp.shape[1])).reshape(xp.shape)
    y_sample = final_norm(xs.reshape(-1, D), final_norm_g, _row_tile(xs.shape[0] * xs.shape[1])).reshape(xs.shape)
    return (y_prompt, y_sample) + tuple(jnp.stack(o) for o in outs)
```

```python
import functools
import math

import jax
import jax.numpy as jnp
from jax import lax
from jax.experimental import pallas as pl
from jax.experimental.pallas import tpu as pltpu

F32 = jnp.float32
BF16 = jnp.bfloat16

EPS = 1e-6
CHUNK = 64
GLA_HEADS, GLA_DK, GLA_DV = 4, 32, 64
GLA_GATE_RANK = 16
GLA_TAU = 16.0
GLA_QK = GLA_HEADS * GLA_DK
GLA_V = GLA_HEADS * GLA_DV
POOL_WINDOWS = (2, 4, 8, 16)
POOL_GROUP = 64
POOL_WIDTH = 256
POOL_HIST = 15
POOL_CARRY = 16
DIFF_HEADS, DIFF_DH, DIFF_DV = 4, 64, 128
DIFF_QK = DIFF_HEADS * 2 * DIFF_DH
DIFF_V = DIFF_HEADS * DIFF_DV
MEM_HEADS, MEM_DH = 4, 128
N_GROUPS, EXPERTS_PER_GROUP, N_EXPERTS = 4, 4, 16
LANES = 128
SUBLANES = 8
GLA_PACK = 2 * GLA_QK + 2 * GLA_V + LANES
ROUTER_E0 = N_GROUPS
VMEM_LIMIT = 56 * 1024 * 1024


def _cparams(sem, **kw):
    return pltpu.CompilerParams(dimension_semantics=sem, vmem_limit_bytes=VMEM_LIMIT, **kw)


def _rms(x, g):
    ms = jnp.mean(x * x, axis=-1, keepdims=True)
    return x * lax.rsqrt(ms + EPS) * g


def _dot(a, b):
    return jnp.dot(a, b, preferred_element_type=F32)


def _dot_nt(a, b):
    return lax.dot_general(a, b, (((1,), (1,)), ((), ())), preferred_element_type=F32)


def _dot_tn(a, b):
    return lax.dot_general(a, b, (((0,), (0,)), ((), ())), preferred_element_type=F32)


N_PROJ_IN = 3


def _norm_proj_body(*refs, plan, n_stacked, tm):
    x_ref, g_ref, w_ref = refs[:N_PROJ_IN]
    o_refs = refs[N_PROJ_IN + n_stacked:]
    h = _rms(x_ref[...], g_ref[...]).astype(BF16)
    cols = {}
    for o_ref, (kind, a, b, dt) in zip(o_refs, plan):
        if (a, b) not in cols:
            cols[(a, b)] = _dot(h, w_ref[:, a:b])
        z = cols[(a, b)]
        if kind == 'rows':
            o_ref[...] = z.astype(dt)
        else:
            nh = (b - a) // LANES
            for hh in range(nh):
                o_ref[pl.ds(hh, tm, stride=nh), :] = z[:, hh * LANES:(hh + 1) * LANES].astype(dt)


def norm_proj(x, g, w, plan, tm, layer=0, stacked=()):
    T, D = x.shape
    out_shape, out_specs, aliases = [], [], {}
    stacked = list(stacked)
    n_stacked = 0
    for k, (kind, a, b, dt) in enumerate(plan):
        if kind == 'rows':
            out_shape.append(jax.ShapeDtypeStruct((T, b - a), dt))
            out_specs.append(pl.BlockSpec((tm, b - a), lambda i: (i, 0)))
        else:
            nh = (b - a) // LANES
            buf = stacked[n_stacked]
            out_shape.append(jax.ShapeDtypeStruct(buf.shape, buf.dtype))
            out_specs.append(pl.BlockSpec((None, tm * nh, LANES), lambda i: (layer, i, 0)))
            aliases[N_PROJ_IN + n_stacked] = k
            n_stacked += 1
    return pl.pallas_call(
        functools.partial(_norm_proj_body, plan=plan, n_stacked=n_stacked, tm=tm),
        grid=(T // tm,),
        in_specs=[pl.BlockSpec((tm, D), lambda i: (i, 0)),
                  pl.BlockSpec((1, D), lambda i: (0, 0)),
                  pl.BlockSpec(w.shape, lambda i: (0, 0))]
                 + [pl.BlockSpec(memory_space=pl.ANY)] * n_stacked,
        out_specs=out_specs, out_shape=out_shape,
        input_output_aliases=aliases,
        compiler_params=_cparams(("parallel",)),
    )(x, g.reshape(1, D), w, *stacked)


def _gla_body(zg_ref, wa_ref, ba_ref, gn_ref, s0_ref, o_ref, sout_ref, s_scr, *, n_chunks):
    C = CHUNK
    i = pl.program_id(1)

    @pl.when(i == 0)
    def _():
        s_scr[...] = s0_ref[...]

    r_i = lax.broadcasted_iota(jnp.int32, (C, C), 0)
    c_i = lax.broadcasted_iota(jnp.int32, (C, C), 1)
    causal = c_i <= r_i
    tri = causal.astype(F32)
    causal_h = (lax.broadcasted_iota(jnp.int32, (GLA_HEADS * C, C), 1)
                <= lax.broadcasted_iota(jnp.int32, (GLA_HEADS * C, C), 0) % C)
    head_k = lax.broadcasted_iota(jnp.int32, (C, GLA_QK), 1) // GLA_DK
    head_v = lax.broadcasted_iota(jnp.int32, (C, GLA_V), 1) // GLA_DV
    st_mask = (lax.broadcasted_iota(jnp.int32, (GLA_V, GLA_QK), 0) // GLA_DV
               == lax.broadcasted_iota(jnp.int32, (GLA_V, GLA_QK), 1) // GLA_DK)

    st = s_scr[...]
    for c in range(n_chunks):
        rows = slice(c * C, (c + 1) * C)
        q = zg_ref[rows, 0:GLA_QK] * (GLA_DK ** -0.5)
        k = zg_ref[rows, GLA_QK:2 * GLA_QK]
        v = zg_ref[rows, 2 * GLA_QK:2 * GLA_QK + GLA_V].astype(BF16)
        gr = zg_ref[rows, 2 * GLA_QK + GLA_V:2 * GLA_QK + 2 * GLA_V]
        ga = zg_ref[rows, 2 * GLA_QK + 2 * GLA_V:GLA_PACK].astype(BF16)
        x = _dot(ga, wa_ref[...]) + ba_ref[...]
        log_a = (jnp.minimum(x, 0.0) - jnp.log1p(jnp.exp(-jnp.abs(x)))) / GLA_TAU
        b = jnp.dot(tri, log_a, preferred_element_type=F32, precision=lax.Precision.HIGHEST)
        b_mid = b[C // 2 - 1:C // 2, :]
        b_end = b[C - 1:C, :]
        q_in = q * jnp.exp(b - b_mid)
        k_in = (k * jnp.exp(b_mid - b)).astype(BF16)
        q_st = jnp.concatenate([jnp.where(head_k == h, q_in, 0.0) for h in range(GLA_HEADS)], axis=0).astype(BF16)
        att = jnp.where(causal_h, _dot_nt(q_st, k_in), 0.0).astype(BF16)
        r = _dot(att, v)
        o = _dot_nt((q * jnp.exp(b)).astype(BF16), st.astype(BF16))
        for h in range(GLA_HEADS):
            o = o + jnp.where(head_v == h, r[h * C:(h + 1) * C], 0.0)
        k_dec = (k * jnp.exp(b_end - b)).astype(BF16)
        st = st * jnp.exp(b_end) + jnp.where(st_mask, _dot_tn(v, k_dec), 0.0)
        o2 = o * o
        ms = jnp.zeros_like(o)
        for h in range(GLA_HEADS):
            mh = jnp.sum(jnp.where(head_v == h, o2, 0.0), axis=-1, keepdims=True) * (1.0 / GLA_DV)
            ms = jnp.where(head_v == h, mh, ms)
        y = o * lax.rsqrt(ms + EPS) * gn_ref[...]
        y = y * (gr * (1.0 / (1.0 + jnp.exp(-gr))))
        o_ref[rows, :] = y.astype(o_ref.dtype)
    s_scr[...] = st

    @pl.when(i == pl.num_programs(1) - 1)
    def _():
        sout_ref[...] = st


def gla_mixer(zg, w_a2, b_a, norm_g, s0):
    B, L, _ = zg.shape
    tg = min(512, L)
    st0 = jnp.zeros((B, GLA_V, GLA_QK), F32)
    for h in range(GLA_HEADS):
        st0 = st0.at[:, h * GLA_DV:(h + 1) * GLA_DV, h * GLA_DK:(h + 1) * GLA_DK].set(
            jnp.swapaxes(s0[:, h].astype(F32), 1, 2))
    wa = jnp.zeros((LANES, GLA_QK), F32).at[:GLA_GATE_RANK].set(w_a2).astype(BF16)
    o, st = pl.pallas_call(
        functools.partial(_gla_body, n_chunks=tg // CHUNK),
        grid=(B, L // tg),
        in_specs=[pl.BlockSpec((None, tg, GLA_PACK), lambda b, i: (b, i, 0)),
                  pl.BlockSpec((LANES, GLA_QK), lambda b, i: (0, 0)),
                  pl.BlockSpec((1, GLA_QK), lambda b, i: (0, 0)),
                  pl.BlockSpec((1, GLA_V), lambda b, i: (0, 0)),
                  pl.BlockSpec((None, GLA_V, GLA_QK), lambda b, i: (b, 0, 0))],
        out_specs=[pl.BlockSpec((None, tg, GLA_V), lambda b, i: (b, i, 0)),
                   pl.BlockSpec((None, GLA_V, GLA_QK), lambda b, i: (b, 0, 0))],
        out_shape=[jax.ShapeDtypeStruct((B, L, GLA_V), BF16),
                   jax.ShapeDtypeStruct((B, GLA_V, GLA_QK), F32)],
        scratch_shapes=[pltpu.VMEM((GLA_V, GLA_QK), F32)],
        compiler_params=_cparams(("parallel", "arbitrary")),
    )(zg, wa, b_a.reshape(1, GLA_QK), jnp.tile(norm_g, GLA_HEADS).reshape(1, GLA_V), st0)
    s_fin = jnp.stack([jnp.swapaxes(st[:, h * GLA_DV:(h + 1) * GLA_DV, h * GLA_DK:(h + 1) * GLA_DK], 1, 2)
                       for h in range(GLA_HEADS)], axis=1)
    return o, s_fin


def _pool_body(u_ref, hist_ref, w_ref, sc_ref, o_ref, buf_ref, carry, *, pos0, tp):
    i = pl.program_id(1)

    @pl.when(i == 0)
    def _():
        carry[...] = hist_ref[...]

    u = u_ref[...]
    ext = jnp.concatenate([carry[...], u], axis=0)
    s2 = ext + pltpu.roll(ext, 1, 0)
    s4 = s2 + pltpu.roll(s2, 2, 0)
    s8 = s4 + pltpu.roll(s4, 4, 0)
    s16 = s8 + pltpu.roll(s8, 8, 0)
    grp = lax.broadcasted_iota(jnp.int32, (tp, POOL_WIDTH), 1) // POOL_GROUP
    pos = pos0 + i * tp + lax.broadcasted_iota(jnp.int32, (tp, POOL_WIDTH), 0)
    win = s16[POOL_CARRY:]
    width = jnp.full((tp, POOL_WIDTH), POOL_WINDOWS[3], jnp.int32)
    for gi, s in ((2, s8), (1, s4), (0, s2)):
        win = jnp.where(grp == gi, s[POOL_CARRY:], win)
        width = jnp.where(grp == gi, POOL_WINDOWS[gi], width)
    cnt = jnp.minimum(width, pos + 1).astype(F32)
    m = win / cnt - u
    o_ref[...] = (_dot(m.astype(BF16), w_ref[...]) * sc_ref[...]).astype(o_ref.dtype)
    tail = ext[tp:]
    carry[...] = tail

    @pl.when(i == pl.num_programs(1) - 1)
    def _():
        buf_ref[...] = tail


def pool_mixer(u, hist, pos0, w_pool, scale):
    B, L, W = u.shape
    tp = min(512, L)
    hist_p = jnp.pad(hist.astype(F32), ((0, 0), (POOL_CARRY - POOL_HIST, 0), (0, 0)))
    wbd = jnp.zeros((W, W), F32)
    for gi in range(len(POOL_WINDOWS)):
        lo = gi * POOL_GROUP
        wbd = wbd.at[lo:lo + POOL_GROUP, lo:lo + POOL_GROUP].set(w_pool[gi])
    o, buf = pl.pallas_call(
        functools.partial(_pool_body, pos0=pos0, tp=tp),
        grid=(B, L // tp),
        in_specs=[pl.BlockSpec((None, tp, W), lambda b, i: (b, i, 0)),
                  pl.BlockSpec((None, POOL_CARRY, W), lambda b, i: (b, 0, 0)),
                  pl.BlockSpec((W, W), lambda b, i: (0, 0)),
                  pl.BlockSpec((1, W), lambda b, i: (0, 0))],
        out_specs=[pl.BlockSpec((None, tp, W), lambda b, i: (b, i, 0)),
                   pl.BlockSpec((None, POOL_CARRY, W), lambda b, i: (b, 0, 0))],
        out_shape=[jax.ShapeDtypeStruct((B, L, W), BF16),
                   jax.ShapeDtypeStruct((B, POOL_CARRY, W), F32)],
        scratch_shapes=[pltpu.VMEM((POOL_CARRY, W), F32)],
        compiler_params=_cparams(("parallel", "arbitrary")),
    )(u, hist_p, wbd.astype(BF16), scale.reshape(1, W))
    return o, buf[:, POOL_CARRY - POOL_HIST:]


def _lam(lq1, lk1, lq2, lk2, lam_init):
    return (jnp.exp(jnp.sum(lq1[...] * lk1[...], axis=-1, keepdims=True))
            - jnp.exp(jnp.sum(lq2[...] * lk2[...], axis=-1, keepdims=True)) + lam_init)


def _half_masks(n):
    lane = lax.broadcasted_iota(jnp.int32, (n, 2 * DIFF_DH), 1)
    return lane < DIFF_DH, lane >= DIFF_DH


def _flash_update(s, v, m_ref, l_ref, acc_ref, idx):
    tk = s.shape[1]
    m_prev = m_ref[idx]
    m_new = jnp.maximum(m_prev, jnp.max(s, axis=-1, keepdims=True))
    alpha = jnp.exp(m_prev - m_new)
    m_cols = jnp.concatenate([m_new] * (tk // LANES), axis=1) if tk % LANES == 0 else m_new[:, :tk]
    p = jnp.exp(s - m_cols)
    l_ref[idx] = alpha * l_ref[idx] + jnp.sum(p, axis=-1, keepdims=True)
    acc_ref[idx] = alpha * acc_ref[idx] + _dot(p.astype(BF16), v)
    m_ref[idx] = m_new


def _diff_finish(acc_ref, l_ref, i0, i1, lam, g, post_scale):
    o = acc_ref[i0] / l_ref[i0] - lam * (acc_ref[i1] / l_ref[i1])
    return _rms(o, g) * post_scale


def _diff_prompt_body(q_ref, k_ref, v_ref, lq1, lk1, lq2, lk2, g_ref, o_ref, m_ref, l_ref, acc_ref,
                      *, tq, lam_init):
    qi = pl.program_id(1)
    lam = _lam(lq1, lk1, lq2, lk2, lam_init)
    keep = _half_masks(tq)
    r_c = lax.broadcasted_iota(jnp.int32, (tq, tq), 0) // CHUNK
    c_c = lax.broadcasted_iota(jnp.int32, (tq, tq), 1) // CHUNK
    visible = c_c <= r_c
    for h in range(DIFF_HEADS):
        cols = slice(h * DIFF_DV, (h + 1) * DIFF_DV)
        q_h = q_ref[:, cols].astype(F32) * (DIFF_DH ** -0.5)
        q_m = [jnp.where(keep[mm], q_h, 0.0).astype(BF16) for mm in range(2)]
        m_ref[...] = jnp.full(m_ref.shape, -jnp.inf, F32)
        l_ref[...] = jnp.zeros(l_ref.shape, F32)
        acc_ref[...] = jnp.zeros(acc_ref.shape, F32)

        def block(kb, masked):
            ks = pl.ds(pl.multiple_of(kb * tq, tq), tq)
            k_h = k_ref[ks, cols]
            v_h = v_ref[ks, cols]
            for mm in range(2):
                s = _dot_nt(q_m[mm], k_h)
                if masked:
                    s = jnp.where(visible, s, -jnp.inf)
                _flash_update(s, v_h, m_ref, l_ref, acc_ref, mm)

        def body(kb, carry):
            block(kb, False)
            return carry

        lax.fori_loop(0, qi, body, 0)
        block(qi, True)
        y = _diff_finish(acc_ref, l_ref, 0, 1, lam, g_ref[...], 1.0 - lam_init)
        o_ref[:, cols] = y.astype(o_ref.dtype)


def diff_attention_prompt(q, k, v, lam_vecs, norm_g, lam_init):
    B, L, _ = q.shape
    tq = min(512, L)
    vec = pl.BlockSpec((1, DIFF_DH), lambda b, i: (0, 0))
    return pl.pallas_call(
        functools.partial(_diff_prompt_body, tq=tq, lam_init=lam_init),
        grid=(B, L // tq),
        in_specs=[pl.BlockSpec((None, tq, DIFF_QK), lambda b, i: (b, i, 0)),
                  pl.BlockSpec((None, L, DIFF_QK), lambda b, i: (b, 0, 0)),
                  pl.BlockSpec((None, L, DIFF_V), lambda b, i: (b, 0, 0)),
                  vec, vec, vec, vec,
                  pl.BlockSpec((1, DIFF_DV), lambda b, i: (0, 0))],
        out_specs=pl.BlockSpec((None, tq, DIFF_V), lambda b, i: (b, i, 0)),
        out_shape=jax.ShapeDtypeStruct((B, L, DIFF_V), BF16),
        scratch_shapes=[pltpu.VMEM((2, tq, LANES), F32), pltpu.VMEM((2, tq, LANES), F32),
                        pltpu.VMEM((2, tq, DIFF_DV), F32)],
        compiler_params=_cparams(("parallel", "arbitrary")),
    )(q, k, v, *[t.reshape(1, DIFF_DH) for t in lam_vecs], norm_g.reshape(1, DIFF_DV))


def _diff_sample_body(q_ref, kp_ref, vp_ref, kn_ref, vn_ref, lq1, lk1, lq2, lk2, g_ref, o_ref,
                      m_ref, l_ref, acc_ref, *, lam_init):
    j = pl.program_id(1)
    last = pl.num_programs(1) - 1
    lq = q_ref.shape[0]
    keep = _half_masks(lq)

    @pl.when(j == 0)
    def _():
        m_ref[...] = jnp.full(m_ref.shape, -jnp.inf, F32)
        l_ref[...] = jnp.zeros(l_ref.shape, F32)
        acc_ref[...] = jnp.zeros(acc_ref.shape, F32)

    def step(head_rows):
        for h in range(DIFF_HEADS):
            q_h = q_ref[:, h * DIFF_DV:(h + 1) * DIFF_DV].astype(F32) * (DIFF_DH ** -0.5)
            k_h, v_h = head_rows(h)
            for mm in range(2):
                q_m = jnp.where(keep[mm], q_h, 0.0).astype(BF16)
                _flash_update(_dot_nt(q_m, k_h), v_h, m_ref, l_ref, acc_ref, 2 * h + mm)

    def cached(h):
        rows = pl.ds(h, kp_ref.shape[0] // DIFF_HEADS, stride=DIFF_HEADS)
        return kp_ref[rows, :].astype(BF16), vp_ref[rows, :].astype(BF16)

    def fresh(h):
        cols = slice(h * DIFF_DV, (h + 1) * DIFF_DV)
        return kn_ref[:, cols], vn_ref[:, cols]

    @pl.when(j < last)
    def _():
        step(cached)

    @pl.when(j == last)
    def _():
        step(fresh)
        lam = _lam(lq1, lk1, lq2, lk2, lam_init)
        for h in range(DIFF_HEADS):
            y = _diff_finish(acc_ref, l_ref, 2 * h, 2 * h + 1, lam, g_ref[...], 1.0 - lam_init)
            o_ref[:, h * DIFF_DV:(h + 1) * DIFF_DV] = y.astype(o_ref.dtype)


def diff_attention_sample(q, k_past, v_past, layer, k_new, v_new, lam_vecs, norm_g, lam_init):
    B, lq, _ = q.shape
    P = k_past.shape[2] // DIFF_HEADS
    tk = min(1024, P)
    n_past = P // tk
    vec = pl.BlockSpec((1, DIFF_DH), lambda b, j: (0, 0))
    past = pl.BlockSpec((None, None, tk * DIFF_HEADS, LANES),
                        lambda b, j: (layer, b, jnp.minimum(j, n_past - 1), 0))
    new = pl.BlockSpec((None, lq, DIFF_QK), lambda b, j: (b, 0, 0))
    n_state = 2 * DIFF_HEADS
    return pl.pallas_call(
        functools.partial(_diff_sample_body, lam_init=lam_init),
        grid=(B, n_past + 1),
        in_specs=[new, past, past, new, new, vec, vec, vec, vec,
                  pl.BlockSpec((1, DIFF_DV), lambda b, j: (0, 0))],
        out_specs=pl.BlockSpec((None, lq, DIFF_V), lambda b, j: (b, 0, 0)),
        out_shape=jax.ShapeDtypeStruct((B, lq, DIFF_V), BF16),
        scratch_shapes=[pltpu.VMEM((n_state, lq, LANES), F32), pltpu.VMEM((n_state, lq, LANES), F32),
                        pltpu.VMEM((n_state, lq, DIFF_DV), F32)],
        compiler_params=_cparams(("parallel", "arbitrary")),
    )(q, k_past, v_past, k_new, v_new, *[t.reshape(1, DIFF_DH) for t in lam_vecs], norm_g.reshape(1, DIFF_DV))


def _out_proj_body(x_ref, og_ref, op_ref, od_ref, w_ref, o_ref):
    y = _dot(og_ref[...], w_ref[0:GLA_V, :])
    y = y + _dot(op_ref[...], w_ref[GLA_V:GLA_V + POOL_WIDTH, :])
    y = y + _dot(od_ref[...], w_ref[GLA_V + POOL_WIDTH:, :])
    o_ref[...] = x_ref[...] + y


def out_proj(x, og, op, od, w, tm):
    T, D = x.shape
    row = lambda n: pl.BlockSpec((tm, n), lambda i: (i, 0))
    return pl.pallas_call(
        _out_proj_body,
        grid=(T // tm,),
        in_specs=[row(D), row(GLA_V), row(POOL_WIDTH), row(DIFF_V), pl.BlockSpec(w.shape, lambda i: (0, 0))],
        out_specs=row(D), out_shape=jax.ShapeDtypeStruct((T, D), F32),
        compiler_params=_cparams(("parallel",)),
    )(x, og, op, od, w)


def _mem_attn_body(x_ref, g_ref, wq_ref, mk_ref, mv_ref, wo_ref, o_ref):
    x = x_ref[...]
    h = _rms(x, g_ref[...]).astype(BF16)
    q = _dot(h, wq_ref[...]).astype(BF16)
    outs = []
    for hd in range(MEM_HEADS):
        cols = slice(hd * MEM_DH, (hd + 1) * MEM_DH)
        s = _dot_nt(q[:, cols], mk_ref[:, cols].astype(BF16)) * (MEM_DH ** -0.5)
        e = jnp.exp(s - jnp.max(s, axis=-1, keepdims=True))
        p = (e / jnp.sum(e, axis=-1, keepdims=True)).astype(BF16)
        outs.append(_dot(p, mv_ref[:, cols].astype(BF16)).astype(BF16))
    o = jnp.concatenate(outs, axis=-1)
    o_ref[...] = x + _dot(o, wo_ref[...])


def mem_attention(x, g, wq, mk, mv, wo):
    B, L, D = x.shape
    M, HD = mk.shape[1], mk.shape[2]
    tm = min(512, L)
    const = lambda shape: pl.BlockSpec(shape, lambda b, i: (0,) * len(shape))
    return pl.pallas_call(
        _mem_attn_body,
        grid=(B, L // tm),
        in_specs=[pl.BlockSpec((None, tm, D), lambda b, i: (b, i, 0)),
                  const((1, D)), const(wq.shape),
                  pl.BlockSpec((None, M, HD), lambda b, i: (b, 0, 0)),
                  pl.BlockSpec((None, M, HD), lambda b, i: (b, 0, 0)),
                  const(wo.shape)],
        out_specs=pl.BlockSpec((None, tm, D), lambda b, i: (b, i, 0)),
        out_shape=jax.ShapeDtypeStruct((B, L, D), F32),
        compiler_params=_cparams(("parallel", "parallel")),
    )(x, g.reshape(1, D), wq, mk, mv, wo)


ROUTER_ROWS = 8


def _router_body(x_ref, g_ref, wt_ref, bt_ref, grp_ref):
    h = _rms(x_ref[...], g_ref[...]).astype(BF16)
    lt = _dot_nt(wt_ref[...], h) + bt_ref[...]
    row = lax.broadcasted_iota(jnp.int32, lt.shape, 0)
    lg = jnp.where(row < N_GROUPS, lt, -jnp.inf)
    g_max = jnp.max(lg, axis=0, keepdims=True)
    grp_ref[...] = jnp.min(jnp.where(lg == g_max, row, ROUTER_ROWS), axis=0, keepdims=True)


def route_groups(x, g, w_group, b_group, tm):
    T, D = x.shape
    wt = jnp.zeros((ROUTER_ROWS, D), F32).at[:N_GROUPS].set(w_group.T).astype(BF16)
    bt = jnp.zeros((ROUTER_ROWS, 1), F32).at[:N_GROUPS, 0].set(b_group)
    grp = pl.pallas_call(
        _router_body,
        grid=(T // tm,),
        in_specs=[pl.BlockSpec((tm, D), lambda i: (i, 0)),
                  pl.BlockSpec((1, D), lambda i: (0, 0)),
                  pl.BlockSpec((ROUTER_ROWS, D), lambda i: (0, 0)),
                  pl.BlockSpec((ROUTER_ROWS, 1), lambda i: (0, 0))],
        out_specs=pl.BlockSpec((None, 1, tm), lambda i: (i, 0, 0)),
        out_shape=jax.ShapeDtypeStruct((T // tm, 1, tm), jnp.int32),
        compiler_params=_cparams(("parallel",)),
    )(x, g.reshape(1, D), wt, bt)
    return grp.reshape(T)


def _dispatch_plan(grp, tm):
    T = grp.shape[0]
    n_tiles = T // tm + N_GROUPS
    tok_sorted = jnp.sort(grp * T + jnp.arange(T, dtype=jnp.int32)) % T
    gids = jnp.arange(N_GROUPS, dtype=jnp.int32)
    counts = jnp.sum((grp[None, :] == gids[:, None]).astype(jnp.int32), axis=1)
    tiles = (counts + tm - 1) // tm
    tile_end = jnp.cumsum(tiles)
    tile_start = tile_end - tiles
    row_start = jnp.cumsum(counts) - counts
    i = jnp.arange(n_tiles, dtype=jnp.int32)
    tile_grp = jnp.minimum(jnp.sum((i[:, None] >= tile_end[None, :]).astype(jnp.int32), axis=1), N_GROUPS - 1)
    local = i - tile_start[tile_grp]
    n_valid = jnp.clip(counts[tile_grp] - local * tm, 0, tm)
    idx = jnp.clip((row_start[tile_grp] + local * tm)[:, None] + jnp.arange(tm, dtype=jnp.int32)[None, :], 0, T - 1)
    return tile_grp.astype(jnp.int32), n_valid.astype(jnp.int32), tok_sorted[idx].reshape(n_tiles, 1, tm)


def _gates_in_group(logits, g):
    lane = lax.broadcasted_iota(jnp.int32, logits.shape, 1)
    big = jnp.int32(LANES)
    neg = -jnp.inf
    lg = jnp.where(lane < N_GROUPS, logits, neg)
    g_max = jnp.max(lg, axis=-1, keepdims=True)
    l_g = jnp.sum(jnp.where(lane == g, logits, 0.0), axis=-1, keepdims=True)
    g_w = jnp.exp(l_g - g_max) / jnp.sum(jnp.exp(lg - g_max), axis=-1, keepdims=True)
    e0 = ROUTER_E0 + g * EXPERTS_PER_GROUP
    in_grp = (lane >= e0) & (lane < e0 + EXPERTS_PER_GROUP)
    le = jnp.where(in_grp, logits, neg)
    e_max = jnp.max(le, axis=-1, keepdims=True)
    pe = jnp.exp(le - e_max)
    pe = pe / jnp.sum(pe, axis=-1, keepdims=True)
    v1 = jnp.max(pe, axis=-1, keepdims=True)
    i1 = jnp.min(jnp.where(pe == v1, lane, big), axis=-1, keepdims=True)
    pe2 = jnp.where((lane == i1) | ~in_grp, -1.0, pe)
    v2 = jnp.max(pe2, axis=-1, keepdims=True)
    i2 = jnp.min(jnp.where(pe2 == v2, lane, big), axis=-1, keepdims=True)
    tot = v1 + v2
    return jnp.where(lane == i1, g_w * (v1 / tot), jnp.where(lane == i2, g_w * (v2 / tot), 0.0))


def _for_rows(nrows, fn):
    n_grp = nrows // SUBLANES

    def group(i, c):
        for u in range(SUBLANES):
            fn(i * SUBLANES + u)
        return c
    lax.fori_loop(0, n_grp, group, 0)

    def single(r, c):
        fn(r)
        return c
    lax.fori_loop(n_grp * SUBLANES, nrows, single, 0)


def _moe_body(tg_ref, nv_ref, tok_ref, tok_next_ref, x_hbm, g_ref, wr_ref, br_ref, wg_ref, wu_ref, wd_ref,
              o_hbm, xbuf, obuf, gsem, ssem):
    i = pl.program_id(0)
    n = pl.num_programs(0)
    slot = i % 2

    def gathered(nrows, s):
        return pltpu.make_async_copy(x_hbm.at[pl.ds(0, nrows)], xbuf.at[s, pl.ds(0, nrows)], gsem.at[s])

    def scattered(nrows, s):
        return pltpu.make_async_copy(obuf.at[s, pl.ds(0, nrows)], o_hbm.at[pl.ds(0, nrows)], ssem.at[s])

    def wait_rows(rows, nrows, s):
        n8 = pl.multiple_of((nrows // SUBLANES) * SUBLANES, SUBLANES)

        @pl.when(n8 > 0)
        def _():
            rows(n8, s).wait()

        def body(r, c):
            rows(1, s).wait()
            return c
        lax.fori_loop(n8, nrows, body, 0)

    def gather(toks, nrows, s):
        def row(r):
            pltpu.make_async_copy(x_hbm.at[pl.ds(toks[0, r], 1)], xbuf.at[s, pl.ds(r, 1)], gsem.at[s]).start()
        _for_rows(nrows, row)

    @pl.when(i == 0)
    def _():
        xbuf[...] = jnp.zeros(xbuf.shape, F32)
        gather(tok_ref, nv_ref[0], 0)

    @pl.when(i + 1 < n)
    def _():
        gather(tok_next_ref, nv_ref[jnp.minimum(i + 1, n - 1)], 1 - slot)

    nv = nv_ref[i]
    nv_prev2 = nv_ref[jnp.maximum(i - 2, 0)]

    @pl.when(i >= 2)
    def _():
        wait_rows(scattered, nv_prev2, slot)

    @pl.when(nv > 0)
    def _():
        wait_rows(gathered, nv, slot)
        g = tg_ref[i]
        x = xbuf[slot]
        h = _rms(x, g_ref[...]).astype(BF16)
        gates = _gates_in_group(_dot(h, wr_ref[...]) + br_ref[...], g)
        lane = lax.broadcasted_iota(jnp.int32, gates.shape, 1)
        acc = x
        for j in range(EXPERTS_PER_GROUP):
            a = _dot(h, wg_ref[j])
            a = a * (1.0 / (1.0 + jnp.exp(-a))) * _dot(h, wu_ref[j])
            y = _dot(a.astype(BF16), wd_ref[j])
            gj = jnp.sum(jnp.where(lane == ROUTER_E0 + g * EXPERTS_PER_GROUP + j, gates, 0.0), axis=-1, keepdims=True)
            acc = acc + gj * y
        obuf[slot] = acc

        def row(r):
            pltpu.make_async_copy(obuf.at[slot, pl.ds(r, 1)], o_hbm.at[pl.ds(tok_ref[0, r], 1)], ssem.at[slot]).start()
        _for_rows(nv, row)

    @pl.when(i == n - 1)
    def _():
        @pl.when(i >= 1)
        def _():
            wait_rows(scattered, nv_ref[jnp.maximum(i - 1, 0)], 1 - slot)

        wait_rows(scattered, nv, slot)


def moe_sparse(x, g, w_group, b_group, wr, br, wg, wu, wd, tm):
    T, D = x.shape
    Hd = wg.shape[-1]
    tile_grp, n_valid, tok = _dispatch_plan(route_groups(x, g, w_group, b_group, tm), tm)
    n_tiles = tok.shape[0]
    const = lambda shape: pl.BlockSpec(shape, lambda i, tg, nv: (0,) * len(shape))
    by_group = lambda shape: pl.BlockSpec((None,) + shape, lambda i, tg, nv: (tg[i], 0, 0, 0))
    return pl.pallas_call(
        _moe_body,
        grid_spec=pltpu.PrefetchScalarGridSpec(
            num_scalar_prefetch=2,
            grid=(n_tiles,),
            in_specs=[pl.BlockSpec((None, 1, tm), lambda i, tg, nv: (i, 0, 0), memory_space=pltpu.SMEM),
                      pl.BlockSpec((None, 1, tm), lambda i, tg, nv: (jnp.minimum(i + 1, n_tiles - 1), 0, 0),
                                   memory_space=pltpu.SMEM),
                      pl.BlockSpec(memory_space=pl.ANY),
                      const((1, D)), const((D, LANES)), const((1, LANES)),
                      by_group((EXPERTS_PER_GROUP, D, Hd)), by_group((EXPERTS_PER_GROUP, D, Hd)),
                      by_group((EXPERTS_PER_GROUP, Hd, D))],
            out_specs=pl.BlockSpec(memory_space=pl.ANY),
            scratch_shapes=[pltpu.VMEM((2, tm, D), F32), pltpu.VMEM((2, tm, D), F32),
                            pltpu.SemaphoreType.DMA((2,)), pltpu.SemaphoreType.DMA((2,))]),
        out_shape=jax.ShapeDtypeStruct((T, D), F32),
        compiler_params=_cparams(("arbitrary",), disable_bounds_checks=True),
    )(tile_grp, n_valid, tok, tok, x, g.reshape(1, D), wr, br, wg, wu, wd)


def _final_norm_body(x_ref, g_ref, o_ref):
    o_ref[...] = _rms(x_ref[...], g_ref[...])


def final_norm(x, g, tm):
    T, D = x.shape
    return pl.pallas_call(
        _final_norm_body,
        grid=(T // tm,),
        in_specs=[pl.BlockSpec((tm, D), lambda i: (i, 0)), pl.BlockSpec((1, D), lambda i: (0, 0))],
        out_specs=pl.BlockSpec((tm, D), lambda i: (i, 0)),
        out_shape=jax.ShapeDtypeStruct((T, D), F32),
        compiler_params=_cparams(("parallel",)),
    )(x, g.reshape(1, D))


_C_PU = GLA_PACK
_C_DQ = _C_PU + POOL_WIDTH
_C_DK = _C_DQ + DIFF_QK
_C_DV = _C_DK + DIFF_QK
_C_END = _C_DV + DIFF_V
_IN_PLAN = (('rows', 0, GLA_PACK, F32), ('rows', _C_PU, _C_DQ, F32),
            ('heads', _C_DK, _C_DV, F32), ('heads', _C_DV, _C_END, F32),
            ('rows', _C_DQ, _C_DK, BF16), ('rows', _C_DK, _C_DV, BF16), ('rows', _C_DV, _C_END, BF16))


def _row_tile(T):
    return min(512, T)


def _layer(p, layer, lam_init, x, gla_s0, pool_hist, kv_past, kv_new, mk, mv):
    B, L, D = x.shape
    T = B * L
    tm = _row_tile(T)
    lam_vecs = (p['lambda_q1'], p['lambda_k1'], p['lambda_q2'], p['lambda_k2'])
    zg, pu, dk, dv, dq, dk16, dv16 = norm_proj(x.reshape(T, D), p['norm_mix_g'], p['w_in'], _IN_PLAN, tm,
                                               layer, kv_new)
    q3, k3, v3 = (t.reshape(B, L, DIFF_QK) for t in (dq, dk16, dv16))
    if kv_past is None:
        P = 0
        o_diff = diff_attention_prompt(q3, k3, v3, lam_vecs, p['diff_norm_g'], lam_init)
    else:
        P = kv_past[0].shape[2] // DIFF_HEADS
        o_diff = diff_attention_sample(q3, kv_past[0], kv_past[1], layer, k3, v3, lam_vecs, p['diff_norm_g'], lam_init)
    o_gla, gla_s = gla_mixer(zg.reshape(B, L, GLA_PACK), p['w_gla_a2'], p['b_gla_a'], p['gla_norm_g'], gla_s0)
    o_pool, pool_buf = pool_mixer(pu.reshape(B, L, POOL_WIDTH), pool_hist, P, p['pool_w'], p['pool_scale'])
    x2 = out_proj(x.reshape(T, D), o_gla.reshape(T, GLA_V), o_pool.reshape(T, POOL_WIDTH),
                  o_diff.reshape(T, DIFF_V), p['w_out'], tm)
    M = mk.shape[1]
    x3 = mem_attention(x2.reshape(B, L, D), p['norm_mem_g'], p['w_mem_q'],
                       mk.reshape(B, M, MEM_HEADS * MEM_DH), mv.reshape(B, M, MEM_HEADS * MEM_DH), p['w_mem_o'])
    x4 = moe_sparse(x3.reshape(T, D), p['norm_ffn_g'], p['w_router_group'], p['b_router_group'],
                    p['w_router'], p['b_router'], p['w_exp_gate'], p['w_exp_up'], p['w_exp_down'], min(tm, T // 8))
    return x4.reshape(B, L, D), gla_s, pool_buf, (dk, dv)


def kernel(x_prompt, x_sample, mem_prompt, cache_diff_k, cache_diff_v, cache_mem_k, cache_mem_v, state_gla, state_pool, norm_mix_g, w_in, w_gla_a2, b_gla_a, gla_norm_g, pool_w, pool_scale, lambda_q1, lambda_k1, lambda_q2, lambda_k2, diff_norm_g, w_out, norm_mem_g, mem_norm_g, w_mem_q, w_mem_kv, w_mem_o, norm_ffn_g, w_router_group, b_router_group, w_router_expert, b_router_expert, w_exp_gate, w_exp_up, w_exp_down, final_norm_g):
    depth = w_in.shape[0]
    D = x_prompt.shape[-1]
    bp, mem_len = mem_prompt.shape[0], mem_prompt.shape[1]
    xp, xs = x_prompt, x_sample
    n_gla_in = 2 * GLA_QK + 2 * GLA_V + GLA_GATE_RANK
    mem_hd = MEM_HEADS * MEM_DH
    grouped = (N_GROUPS, EXPERTS_PER_GROUP)
    outs = [[] for _ in range(6)]
    bs, ls = x_sample.shape[0], x_sample.shape[1]
    lp = x_prompt.shape[1]
    past_len = cache_diff_k.shape[2]
    kv_p = tuple(jnp.zeros((depth, bp * lp * DIFF_HEADS, LANES), F32) for _ in range(2))
    kv_s = tuple(jnp.zeros((depth, bs * ls * DIFF_HEADS, LANES), F32) for _ in range(2))
    kv_cache = tuple(c.reshape(depth, bs, past_len * DIFF_HEADS, LANES) for c in (cache_diff_k, cache_diff_v))
    for l in range(depth):
        wi = w_in[l]
        w_in_p = jnp.concatenate([wi[:, :n_gla_in], jnp.zeros((D, GLA_PACK - n_gla_in), wi.dtype), wi[:, n_gla_in:]],
                                 axis=1).astype(BF16)
        pad = LANES - N_GROUPS - N_EXPERTS
        w_router = jnp.concatenate([w_router_group[l], w_router_expert[l], jnp.zeros((D, pad), F32)], axis=1).astype(BF16)
        b_router = jnp.concatenate([b_router_group[l], b_router_expert[l], jnp.zeros((pad,), F32)]).reshape(1, LANES)
        p = dict(norm_mix_g=norm_mix_g[l], w_in=w_in_p, w_gla_a2=w_gla_a2[l], b_gla_a=b_gla_a[l],
                 gla_norm_g=gla_norm_g[l], pool_w=pool_w[l], pool_scale=pool_scale[l],
                 lambda_q1=lambda_q1[l], lambda_k1=lambda_k1[l], lambda_q2=lambda_q2[l], lambda_k2=lambda_k2[l],
                 diff_norm_g=diff_norm_g[l], w_out=w_out[l].astype(BF16), norm_mem_g=norm_mem_g[l],
                 w_mem_q=w_mem_q[l].astype(BF16), w_mem_o=w_mem_o[l].astype(BF16), norm_ffn_g=norm_ffn_g[l],
                 w_router=w_router, b_router=b_router,
                 w_router_group=w_router_group[l], b_router_group=b_router_group[l],
                 w_exp_gate=w_exp_gate[l].astype(BF16).reshape(grouped + w_exp_gate.shape[2:]),
                 w_exp_up=w_exp_up[l].astype(BF16).reshape(grouped + w_exp_up.shape[2:]),
                 w_exp_down=w_exp_down[l].astype(BF16).reshape(grouped + w_exp_down.shape[2:]))
        lam_init = 0.8 - 0.6 * math.exp(-0.3 * l)
        mk, mv = norm_proj(mem_prompt.reshape(bp * mem_len, D), mem_norm_g[l], w_mem_kv[l].astype(BF16),
                           (('rows', 0, mem_hd, F32), ('rows', mem_hd, 2 * mem_hd, F32)), _row_tile(bp * mem_len))
        mk = mk.reshape(bp, mem_len, MEM_HEADS, MEM_DH)
        mv = mv.reshape(bp, mem_len, MEM_HEADS, MEM_DH)
        xp, g_new, pool_new, kv_p = _layer(
            p, l, lam_init, xp, jnp.zeros((bp, GLA_HEADS, GLA_DK, GLA_DV), F32),
            jnp.zeros((bp, POOL_HIST, POOL_WIDTH), F32), None, kv_p, mk, mv)
        for lst, t in zip(outs[:4], (mk, mv, g_new, pool_new)):
            lst.append(t)
        xs, g_new, pool_new, kv_s = _layer(
            p, l, lam_init, xs, state_gla[l], state_pool[l], kv_cache, kv_s, cache_mem_k[l], cache_mem_v[l])
        for lst, t in zip(outs[4:], (g_new, pool_new)):
            lst.append(t)
    y_prompt = final_norm(xp.reshape(-1, D), final_norm_g, _row_tile(xp.shape[0] * xp.shape[1])).reshape(xp.shape)
    y_sample = final_norm(xs.reshape(-1, D), final_norm_g, _row_tile(xs.shape[0] * xs.shape[1])).reshape(xs.shape)
    mk_p, mv_p, gs_p, ps_p, gs_s, ps_s = (jnp.stack(o) for o in outs)
    kv_shape_p = (depth, bp, lp, DIFF_HEADS, LANES)
    kv_shape_s = (depth, bs, ls, DIFF_HEADS, LANES)
    return (y_prompt, y_sample, kv_p[0].reshape(kv_shape_p), kv_p[1].reshape(kv_shape_p), mk_p, mv_p, gs_p, ps_p,
            kv_s[0].reshape(kv_shape_s), kv_s[1].reshape(kv_shape_s), gs_s, ps_s)
```

```python
import functools
import math

import jax
import jax.numpy as jnp
from jax import lax
from jax.experimental import pallas as pl
from jax.experimental.pallas import tpu as pltpu

F32 = jnp.float32
BF16 = jnp.bfloat16

EPS = 1e-6
CHUNK = 64
GLA_HEADS, GLA_DK, GLA_DV = 4, 32, 64
GLA_GATE_RANK = 16
GLA_TAU = 16.0
GLA_QK = GLA_HEADS * GLA_DK
GLA_V = GLA_HEADS * GLA_DV
POOL_WINDOWS = (2, 4, 8, 16)
POOL_GROUP = 64
POOL_WIDTH = 256
POOL_HIST = 15
POOL_CARRY = 16
DIFF_HEADS, DIFF_DH, DIFF_DV = 4, 64, 128
DIFF_QK = DIFF_HEADS * 2 * DIFF_DH
DIFF_V = DIFF_HEADS * DIFF_DV
MEM_HEADS, MEM_DH = 4, 128
N_GROUPS, EXPERTS_PER_GROUP, N_EXPERTS = 4, 4, 16
LANES = 128
SUBLANES = 8
GLA_PACK = 2 * GLA_QK + 2 * GLA_V + LANES
ROUTER_E0 = N_GROUPS
VMEM_LIMIT = 56 * 1024 * 1024


def _cparams(sem, **kw):
    return pltpu.CompilerParams(dimension_semantics=sem, vmem_limit_bytes=VMEM_LIMIT, **kw)


def _rms(x, g):
    ms = jnp.mean(x * x, axis=-1, keepdims=True)
    return x * lax.rsqrt(ms + EPS) * g


def _dot(a, b):
    return jnp.dot(a, b, preferred_element_type=F32)


def _dot_nt(a, b):
    return lax.dot_general(a, b, (((1,), (1,)), ((), ())), preferred_element_type=F32)


def _dot_tn(a, b):
    return lax.dot_general(a, b, (((0,), (0,)), ((), ())), preferred_element_type=F32)


N_PROJ_IN = 3


def _norm_proj_body(*refs, plan, n_stacked, tm):
    x_ref, g_ref, w_ref = refs[:N_PROJ_IN]
    o_refs = refs[N_PROJ_IN + n_stacked:]
    h = _rms(x_ref[...], g_ref[...]).astype(BF16)
    cols = {}
    for o_ref, (kind, a, b, dt) in zip(o_refs, plan):
        if (a, b) not in cols:
            cols[(a, b)] = _dot(h, w_ref[:, a:b])
        z = cols[(a, b)]
        if kind == 'rows':
            o_ref[...] = z.astype(dt)
        else:
            nh = (b - a) // LANES
            for hh in range(nh):
                o_ref[pl.ds(hh, tm, stride=nh), :] = z[:, hh * LANES:(hh + 1) * LANES].astype(dt)


def norm_proj(x, g, w, plan, tm, layer=0, stacked=()):
    T, D = x.shape
    out_shape, out_specs, aliases = [], [], {}
    stacked = list(stacked)
    n_stacked = 0
    for k, (kind, a, b, dt) in enumerate(plan):
        if kind == 'rows':
            out_shape.append(jax.ShapeDtypeStruct((T, b - a), dt))
            out_specs.append(pl.BlockSpec((tm, b - a), lambda i: (i, 0)))
        else:
            nh = (b - a) // LANES
            buf = stacked[n_stacked]
            out_shape.append(jax.ShapeDtypeStruct(buf.shape, buf.dtype))
            out_specs.append(pl.BlockSpec((None, tm * nh, LANES), lambda i: (layer, i, 0)))
            aliases[N_PROJ_IN + n_stacked] = k
            n_stacked += 1
    return pl.pallas_call(
        functools.partial(_norm_proj_body, plan=plan, n_stacked=n_stacked, tm=tm),
        grid=(T // tm,),
        in_specs=[pl.BlockSpec((tm, D), lambda i: (i, 0)),
                  pl.BlockSpec((1, D), lambda i: (0, 0)),
                  pl.BlockSpec(w.shape, lambda i: (0, 0))]
                 + [pl.BlockSpec(memory_space=pl.ANY)] * n_stacked,
        out_specs=out_specs, out_shape=out_shape,
        input_output_aliases=aliases,
        compiler_params=_cparams(("parallel",)),
    )(x, g.reshape(1, D), w, *stacked)


def _gla_body(zg_ref, wa_ref, ba_ref, gn_ref, s0_ref, o_ref, sout_ref, s_scr, *, n_chunks):
    C = CHUNK
    i = pl.program_id(1)

    @pl.when(i == 0)
    def _():
        s_scr[...] = s0_ref[...]

    rows = [slice(c * C, (c + 1) * C) for c in range(n_chunks)]
    tri = (lax.broadcasted_iota(jnp.int32, (C, C), 1) <= lax.broadcasted_iota(jnp.int32, (C, C), 0)).astype(F32)
    causal_h = (lax.broadcasted_iota(jnp.int32, (GLA_HEADS * C, C), 1)
                <= lax.broadcasted_iota(jnp.int32, (GLA_HEADS * C, C), 0) % C)
    head_k = lax.broadcasted_iota(jnp.int32, (C, GLA_QK), 1) // GLA_DK
    head_v = lax.broadcasted_iota(jnp.int32, (C, GLA_V), 1) // GLA_DV
    st_mask = (lax.broadcasted_iota(jnp.int32, (GLA_V, GLA_QK), 0) // GLA_DV
               == lax.broadcasted_iota(jnp.int32, (GLA_V, GLA_QK), 1) // GLA_DK)

    q = zg_ref[:, 0:GLA_QK] * (GLA_DK ** -0.5)
    k = zg_ref[:, GLA_QK:2 * GLA_QK]
    v = zg_ref[:, 2 * GLA_QK:2 * GLA_QK + GLA_V].astype(BF16)
    ga = zg_ref[:, 2 * GLA_QK + 2 * GLA_V:GLA_PACK].astype(BF16)
    x = _dot(ga, wa_ref[...]) + ba_ref[...]
    log_a = (jnp.minimum(x, 0.0) - jnp.log1p(jnp.exp(-jnp.abs(x)))) / GLA_TAU
    b = jnp.concatenate([jnp.dot(tri, log_a[r], preferred_element_type=F32, precision=lax.Precision.HIGHEST)
                         for r in rows], axis=0)

    def chunk_row(row):
        return jnp.concatenate([jnp.broadcast_to(b[c * C + row:c * C + row + 1], (C, GLA_QK))
                                for c in range(n_chunks)], axis=0)
    b_mid = chunk_row(C // 2 - 1)
    b_end = chunk_row(C - 1)
    q_in = q * jnp.exp(b - b_mid)
    k_in = (k * jnp.exp(b_mid - b)).astype(BF16)
    q_dec = (q * jnp.exp(b)).astype(BF16)
    k_dec = (k * jnp.exp(b_end - b)).astype(BF16)
    o_intra, kv = [], []
    for r in rows:
        q_st = jnp.concatenate([jnp.where(head_k == h, q_in[r], 0.0) for h in range(GLA_HEADS)], axis=0).astype(BF16)
        att = jnp.where(causal_h, _dot_nt(q_st, k_in[r]), 0.0).astype(BF16)
        res = _dot(att, v[r])
        o_c = jnp.where(head_v == 0, res[0:C], 0.0)
        for h in range(1, GLA_HEADS):
            o_c = o_c + jnp.where(head_v == h, res[h * C:(h + 1) * C], 0.0)
        o_intra.append(o_c)
        kv.append(jnp.where(st_mask, _dot_tn(v[r], k_dec[r]), 0.0))
    st = s_scr[...]
    o_parts = []
    for c, r in enumerate(rows):
        o_parts.append(o_intra[c] + _dot_nt(q_dec[r], st.astype(BF16)))
        st = st * jnp.exp(b[(c + 1) * C - 1:(c + 1) * C]) + kv[c]
    s_scr[...] = st
    o = jnp.concatenate(o_parts, axis=0)
    head_o = lax.broadcasted_iota(jnp.int32, o.shape, 1) // GLA_DV
    o2 = o * o
    ms = jnp.zeros_like(o)
    for h in range(GLA_HEADS):
        mh = jnp.sum(jnp.where(head_o == h, o2, 0.0), axis=-1, keepdims=True) * (1.0 / GLA_DV)
        ms = jnp.where(head_o == h, mh, ms)
    gr = zg_ref[:, 2 * GLA_QK + GLA_V:2 * GLA_QK + 2 * GLA_V]
    y = o * lax.rsqrt(ms + EPS) * gn_ref[...]
    o_ref[...] = (y * (gr * (1.0 / (1.0 + jnp.exp(-gr))))).astype(o_ref.dtype)

    @pl.when(i == pl.num_programs(1) - 1)
    def _():
        sout_ref[...] = st


def gla_mixer(zg, w_a2, b_a, norm_g, s0):
    B, L, _ = zg.shape
    tg = min(512, L)
    st0 = jnp.zeros((B, GLA_V, GLA_QK), F32)
    for h in range(GLA_HEADS):
        st0 = st0.at[:, h * GLA_DV:(h + 1) * GLA_DV, h * GLA_DK:(h + 1) * GLA_DK].set(
            jnp.swapaxes(s0[:, h].astype(F32), 1, 2))
    wa = jnp.zeros((LANES, GLA_QK), F32).at[:GLA_GATE_RANK].set(w_a2).astype(BF16)
    o, st = pl.pallas_call(
        functools.partial(_gla_body, n_chunks=tg // CHUNK),
        grid=(B, L // tg),
        in_specs=[pl.BlockSpec((None, tg, GLA_PACK), lambda b, i: (b, i, 0)),
                  pl.BlockSpec((LANES, GLA_QK), lambda b, i: (0, 0)),
                  pl.BlockSpec((1, GLA_QK), lambda b, i: (0, 0)),
                  pl.BlockSpec((1, GLA_V), lambda b, i: (0, 0)),
                  pl.BlockSpec((None, GLA_V, GLA_QK), lambda b, i: (b, 0, 0))],
        out_specs=[pl.BlockSpec((None, tg, GLA_V), lambda b, i: (b, i, 0)),
                   pl.BlockSpec((None, GLA_V, GLA_QK), lambda b, i: (b, 0, 0))],
        out_shape=[jax.ShapeDtypeStruct((B, L, GLA_V), BF16),
                   jax.ShapeDtypeStruct((B, GLA_V, GLA_QK), F32)],
        scratch_shapes=[pltpu.VMEM((GLA_V, GLA_QK), F32)],
        compiler_params=_cparams(("parallel", "arbitrary")),
    )(zg, wa, b_a.reshape(1, GLA_QK), jnp.tile(norm_g, GLA_HEADS).reshape(1, GLA_V), st0)
    s_fin = jnp.stack([jnp.swapaxes(st[:, h * GLA_DV:(h + 1) * GLA_DV, h * GLA_DK:(h + 1) * GLA_DK], 1, 2)
                       for h in range(GLA_HEADS)], axis=1)
    return o, s_fin


def _pool_body(u_ref, hist_ref, w_ref, sc_ref, o_ref, buf_ref, carry, *, pos0, tp):
    i = pl.program_id(1)

    @pl.when(i == 0)
    def _():
        carry[...] = hist_ref[...]

    u = u_ref[...]
    ext = jnp.concatenate([carry[...], u], axis=0)
    s2 = ext + pltpu.roll(ext, 1, 0)
    s4 = s2 + pltpu.roll(s2, 2, 0)
    s8 = s4 + pltpu.roll(s4, 4, 0)
    s16 = s8 + pltpu.roll(s8, 8, 0)
    grp = lax.broadcasted_iota(jnp.int32, (tp, POOL_WIDTH), 1) // POOL_GROUP
    pos = pos0 + i * tp + lax.broadcasted_iota(jnp.int32, (tp, POOL_WIDTH), 0)
    win = s16[POOL_CARRY:]
    width = jnp.full((tp, POOL_WIDTH), POOL_WINDOWS[3], jnp.int32)
    for gi, s in ((2, s8), (1, s4), (0, s2)):
        win = jnp.where(grp == gi, s[POOL_CARRY:], win)
        width = jnp.where(grp == gi, POOL_WINDOWS[gi], width)
    cnt = jnp.minimum(width, pos + 1).astype(F32)
    m = win / cnt - u
    o_ref[...] = (_dot(m.astype(BF16), w_ref[...]) * sc_ref[...]).astype(o_ref.dtype)
    tail = ext[tp:]
    carry[...] = tail

    @pl.when(i == pl.num_programs(1) - 1)
    def _():
        buf_ref[...] = tail


def pool_mixer(u, hist, pos0, w_pool, scale):
    B, L, W = u.shape
    tp = min(512, L)
    hist_p = jnp.pad(hist.astype(F32), ((0, 0), (POOL_CARRY - POOL_HIST, 0), (0, 0)))
    wbd = jnp.zeros((W, W), F32)
    for gi in range(len(POOL_WINDOWS)):
        lo = gi * POOL_GROUP
        wbd = wbd.at[lo:lo + POOL_GROUP, lo:lo + POOL_GROUP].set(w_pool[gi])
    o, buf = pl.pallas_call(
        functools.partial(_pool_body, pos0=pos0, tp=tp),
        grid=(B, L // tp),
        in_specs=[pl.BlockSpec((None, tp, W), lambda b, i: (b, i, 0)),
                  pl.BlockSpec((None, POOL_CARRY, W), lambda b, i: (b, 0, 0)),
                  pl.BlockSpec((W, W), lambda b, i: (0, 0)),
                  pl.BlockSpec((1, W), lambda b, i: (0, 0))],
        out_specs=[pl.BlockSpec((None, tp, W), lambda b, i: (b, i, 0)),
                   pl.BlockSpec((None, POOL_CARRY, W), lambda b, i: (b, 0, 0))],
        out_shape=[jax.ShapeDtypeStruct((B, L, W), BF16),
                   jax.ShapeDtypeStruct((B, POOL_CARRY, W), F32)],
        scratch_shapes=[pltpu.VMEM((POOL_CARRY, W), F32)],
        compiler_params=_cparams(("parallel", "arbitrary")),
    )(u, hist_p, wbd.astype(BF16), scale.reshape(1, W))
    return o, buf[:, POOL_CARRY - POOL_HIST:]


def _lam(lq1, lk1, lq2, lk2, lam_init):
    return (jnp.exp(jnp.sum(lq1[...] * lk1[...], axis=-1, keepdims=True))
            - jnp.exp(jnp.sum(lq2[...] * lk2[...], axis=-1, keepdims=True)) + lam_init)


def _half_masks(n):
    lane = lax.broadcasted_iota(jnp.int32, (n, 2 * DIFF_DH), 1)
    return lane < DIFF_DH, lane >= DIFF_DH


DIFF_Q_SCALE = DIFF_DH ** -0.5 * math.log2(math.e)


def _flash_update(s, v, m_ref, l_ref, acc_ref, idx):
    tk = s.shape[1]
    m_prev = m_ref[idx]
    m_new = jnp.maximum(m_prev, jnp.max(s, axis=-1, keepdims=True))
    alpha = jnp.exp2(m_prev - m_new)
    m_cols = jnp.concatenate([m_new] * (tk // LANES), axis=1) if tk % LANES == 0 else m_new[:, :tk]
    p = jnp.exp2(s - m_cols)
    l_ref[idx] = alpha * l_ref[idx] + jnp.sum(p, axis=-1, keepdims=True)
    acc_ref[idx] = alpha * acc_ref[idx] + _dot(p.astype(BF16), v)
    m_ref[idx] = m_new


def _diff_finish(acc_ref, l_ref, i0, i1, lam, g, post_scale):
    o = acc_ref[i0] / l_ref[i0] - lam * (acc_ref[i1] / l_ref[i1])
    return _rms(o, g) * post_scale


def _diff_prompt_body(q_ref, k_ref, v_ref, lq1, lk1, lq2, lk2, g_ref, o_ref, m_ref, l_ref, acc_ref,
                      *, tq, lam_init):
    qi = pl.program_id(1)
    lam = _lam(lq1, lk1, lq2, lk2, lam_init)
    keep = _half_masks(tq)
    r_c = lax.broadcasted_iota(jnp.int32, (tq, tq), 0) // CHUNK
    c_c = lax.broadcasted_iota(jnp.int32, (tq, tq), 1) // CHUNK
    visible = c_c <= r_c
    for h in range(DIFF_HEADS):
        cols = slice(h * DIFF_DV, (h + 1) * DIFF_DV)
        q_h = q_ref[:, cols].astype(F32) * DIFF_Q_SCALE
        q_m = [jnp.where(keep[mm], q_h, 0.0).astype(BF16) for mm in range(2)]
        m_ref[...] = jnp.full(m_ref.shape, -jnp.inf, F32)
        l_ref[...] = jnp.zeros(l_ref.shape, F32)
        acc_ref[...] = jnp.zeros(acc_ref.shape, F32)

        def block(kb, masked):
            ks = pl.ds(pl.multiple_of(kb * tq, tq), tq)
            k_h = k_ref[ks, cols]
            v_h = v_ref[ks, cols]
            for mm in range(2):
                s = _dot_nt(q_m[mm], k_h)
                if masked:
                    s = jnp.where(visible, s, -jnp.inf)
                _flash_update(s, v_h, m_ref, l_ref, acc_ref, mm)

        def body(kb, carry):
            block(kb, False)
            return carry

        lax.fori_loop(0, qi, body, 0)
        block(qi, True)
        y = _diff_finish(acc_ref, l_ref, 0, 1, lam, g_ref[...], 1.0 - lam_init)
        o_ref[:, cols] = y.astype(o_ref.dtype)


def diff_attention_prompt(q, k, v, lam_vecs, norm_g, lam_init):
    B, L, _ = q.shape
    tq = min(512, L)
    vec = pl.BlockSpec((1, DIFF_DH), lambda b, i: (0, 0))
    return pl.pallas_call(
        functools.partial(_diff_prompt_body, tq=tq, lam_init=lam_init),
        grid=(B, L // tq),
        in_specs=[pl.BlockSpec((None, tq, DIFF_QK), lambda b, i: (b, i, 0)),
                  pl.BlockSpec((None, L, DIFF_QK), lambda b, i: (b, 0, 0)),
                  pl.BlockSpec((None, L, DIFF_V), lambda b, i: (b, 0, 0)),
                  vec, vec, vec, vec,
                  pl.BlockSpec((1, DIFF_DV), lambda b, i: (0, 0))],
        out_specs=pl.BlockSpec((None, tq, DIFF_V), lambda b, i: (b, i, 0)),
        out_shape=jax.ShapeDtypeStruct((B, L, DIFF_V), BF16),
        scratch_shapes=[pltpu.VMEM((2, tq, LANES), F32), pltpu.VMEM((2, tq, LANES), F32),
                        pltpu.VMEM((2, tq, DIFF_DV), F32)],
        compiler_params=_cparams(("parallel", "arbitrary")),
    )(q, k, v, *[t.reshape(1, DIFF_DH) for t in lam_vecs], norm_g.reshape(1, DIFF_DV))


def _diff_sample_body(q_ref, kp_ref, vp_ref, kn_ref, vn_ref, lq1, lk1, lq2, lk2, g_ref, o_ref,
                      m_ref, l_ref, acc_ref, *, lam_init):
    j = pl.program_id(1)
    last = pl.num_programs(1) - 1
    lq = q_ref.shape[0]
    keep = _half_masks(lq)

    @pl.when(j == 0)
    def _():
        m_ref[...] = jnp.full(m_ref.shape, -jnp.inf, F32)
        l_ref[...] = jnp.zeros(l_ref.shape, F32)
        acc_ref[...] = jnp.zeros(acc_ref.shape, F32)

    def step(head_rows):
        for h in range(DIFF_HEADS):
            q_h = q_ref[:, h * DIFF_DV:(h + 1) * DIFF_DV].astype(F32) * DIFF_Q_SCALE
            k_h, v_h = head_rows(h)
            for mm in range(2):
                q_m = jnp.where(keep[mm], q_h, 0.0).astype(BF16)
                _flash_update(_dot_nt(q_m, k_h), v_h, m_ref, l_ref, acc_ref, 2 * h + mm)

    def cached(h):
        rows = pl.ds(h, kp_ref.shape[0] // DIFF_HEADS, stride=DIFF_HEADS)
        return kp_ref[rows, :].astype(BF16), vp_ref[rows, :].astype(BF16)

    def fresh(h):
        cols = slice(h * DIFF_DV, (h + 1) * DIFF_DV)
        return kn_ref[:, cols], vn_ref[:, cols]

    @pl.when(j < last)
    def _():
        step(cached)

    @pl.when(j == last)
    def _():
        step(fresh)
        lam = _lam(lq1, lk1, lq2, lk2, lam_init)
        for h in range(DIFF_HEADS):
            y = _diff_finish(acc_ref, l_ref, 2 * h, 2 * h + 1, lam, g_ref[...], 1.0 - lam_init)
            o_ref[:, h * DIFF_DV:(h + 1) * DIFF_DV] = y.astype(o_ref.dtype)


def diff_attention_sample(q, k_past, v_past, layer, k_new, v_new, lam_vecs, norm_g, lam_init):
    B, lq, _ = q.shape
    P = k_past.shape[2] // DIFF_HEADS
    tk = min(1024, P)
    n_past = P // tk
    vec = pl.BlockSpec((1, DIFF_DH), lambda b, j: (0, 0))
    past = pl.BlockSpec((None, None, tk * DIFF_HEADS, LANES),
                        lambda b, j: (layer, b, jnp.minimum(j, n_past - 1), 0))
    new = pl.BlockSpec((None, lq, DIFF_QK), lambda b, j: (b, 0, 0))
    n_state = 2 * DIFF_HEADS
    return pl.pallas_call(
        functools.partial(_diff_sample_body, lam_init=lam_init),
        grid=(B, n_past + 1),
        in_specs=[new, past, past, new, new, vec, vec, vec, vec,
                  pl.BlockSpec((1, DIFF_DV), lambda b, j: (0, 0))],
        out_specs=pl.BlockSpec((None, lq, DIFF_V), lambda b, j: (b, 0, 0)),
        out_shape=jax.ShapeDtypeStruct((B, lq, DIFF_V), BF16),
        scratch_shapes=[pltpu.VMEM((n_state, lq, LANES), F32), pltpu.VMEM((n_state, lq, LANES), F32),
                        pltpu.VMEM((n_state, lq, DIFF_DV), F32)],
        compiler_params=_cparams(("parallel", "arbitrary")),
    )(q, k_past, v_past, k_new, v_new, *[t.reshape(1, DIFF_DH) for t in lam_vecs], norm_g.reshape(1, DIFF_DV))


ROUTER_ROWS = 8


def _mix_mem_body(x_ref, og_ref, op_ref, od_ref, wout_ref, gm_ref, wq_ref, mk_ref, mv_ref, wo_ref,
                  gf_ref, wt_ref, bt_ref, o_ref, grp_ref):
    y = _dot(og_ref[...], wout_ref[0:GLA_V, :])
    y = y + _dot(op_ref[...], wout_ref[GLA_V:GLA_V + POOL_WIDTH, :])
    y = y + _dot(od_ref[...], wout_ref[GLA_V + POOL_WIDTH:, :])
    x = x_ref[...] + y
    h = _rms(x, gm_ref[...]).astype(BF16)
    q = _dot(h, wq_ref[...]).astype(BF16)
    outs = []
    for hd in range(MEM_HEADS):
        cols = slice(hd * MEM_DH, (hd + 1) * MEM_DH)
        s = _dot_nt(q[:, cols], mk_ref[:, cols].astype(BF16)) * (MEM_DH ** -0.5)
        e = jnp.exp(s - jnp.max(s, axis=-1, keepdims=True))
        p = (e / jnp.sum(e, axis=-1, keepdims=True)).astype(BF16)
        outs.append(_dot(p, mv_ref[:, cols].astype(BF16)).astype(BF16))
    x = x + _dot(jnp.concatenate(outs, axis=-1), wo_ref[...])
    o_ref[...] = x
    hf = _rms(x, gf_ref[...]).astype(BF16)
    lt = _dot_nt(wt_ref[...], hf) + bt_ref[...]
    row = lax.broadcasted_iota(jnp.int32, lt.shape, 0)
    lg = jnp.where(row < N_GROUPS, lt, -jnp.inf)
    g_max = jnp.max(lg, axis=0, keepdims=True)
    grp_ref[...] = jnp.min(jnp.where(lg == g_max, row, ROUTER_ROWS), axis=0, keepdims=True)


def mix_mem(x, og, op, od, w_out, g_mem, wq, mk, mv, wo, g_ffn, w_group, b_group):
    B, L, D = x.shape
    M, HD = mk.shape[1], mk.shape[2]
    tm = min(512, L)
    n_l = L // tm
    wt = jnp.zeros((ROUTER_ROWS, D), F32).at[:N_GROUPS].set(w_group.T).astype(BF16)
    bt = jnp.zeros((ROUTER_ROWS, 1), F32).at[:N_GROUPS, 0].set(b_group)
    const = lambda shape: pl.BlockSpec(shape, lambda b, i: (0,) * len(shape))
    rows = lambda n: pl.BlockSpec((None, tm, n), lambda b, i: (b, i, 0))
    per_b = pl.BlockSpec((None, M, HD), lambda b, i: (b, 0, 0))
    x3, grp = pl.pallas_call(
        _mix_mem_body,
        grid=(B, n_l),
        in_specs=[rows(D), rows(GLA_V), rows(POOL_WIDTH), rows(DIFF_V), const(w_out.shape),
                  const((1, D)), const(wq.shape), per_b, per_b, const(wo.shape),
                  const((1, D)), const((ROUTER_ROWS, D)), const((ROUTER_ROWS, 1))],
        out_specs=[rows(D), pl.BlockSpec((None, 1, tm), lambda b, i: (b * n_l + i, 0, 0))],
        out_shape=[jax.ShapeDtypeStruct((B, L, D), F32), jax.ShapeDtypeStruct((B * n_l, 1, tm), jnp.int32)],
        compiler_params=_cparams(("parallel", "parallel")),
    )(x, og, op, od, w_out, g_mem.reshape(1, D), wq, mk, mv, wo, g_ffn.reshape(1, D), wt, bt)
    return x3, grp.reshape(B * L)


def _dispatch_plan(grp, tm):
    T = grp.shape[0]
    n_tiles = T // tm + N_GROUPS
    tok_sorted = jnp.sort(grp * T + jnp.arange(T, dtype=jnp.int32)) % T
    gids = jnp.arange(N_GROUPS, dtype=jnp.int32)
    counts = jnp.sum((grp[None, :] == gids[:, None]).astype(jnp.int32), axis=1)
    tiles = (counts + tm - 1) // tm
    tile_end = jnp.cumsum(tiles)
    tile_start = tile_end - tiles
    row_start = jnp.cumsum(counts) - counts
    i = jnp.arange(n_tiles, dtype=jnp.int32)
    tile_grp = jnp.minimum(jnp.sum((i[:, None] >= tile_end[None, :]).astype(jnp.int32), axis=1), N_GROUPS - 1)
    local = i - tile_start[tile_grp]
    n_valid = jnp.clip(counts[tile_grp] - local * tm, 0, tm)
    idx = jnp.clip((row_start[tile_grp] + local * tm)[:, None] + jnp.arange(tm, dtype=jnp.int32)[None, :], 0, T - 1)
    return tile_grp.astype(jnp.int32), n_valid.astype(jnp.int32), tok_sorted[idx].reshape(n_tiles, 1, tm)


def _gates_in_group(logits, g):
    lane = lax.broadcasted_iota(jnp.int32, logits.shape, 1)
    big = jnp.int32(LANES)
    neg = -jnp.inf
    lg = jnp.where(lane < N_GROUPS, logits, neg)
    g_max = jnp.max(lg, axis=-1, keepdims=True)
    l_g = jnp.sum(jnp.where(lane == g, logits, 0.0), axis=-1, keepdims=True)
    g_w = jnp.exp(l_g - g_max) / jnp.sum(jnp.exp(lg - g_max), axis=-1, keepdims=True)
    e0 = ROUTER_E0 + g * EXPERTS_PER_GROUP
    in_grp = (lane >= e0) & (lane < e0 + EXPERTS_PER_GROUP)
    le = jnp.where(in_grp, logits, neg)
    e_max = jnp.max(le, axis=-1, keepdims=True)
    pe = jnp.exp(le - e_max)
    pe = pe / jnp.sum(pe, axis=-1, keepdims=True)
    v1 = jnp.max(pe, axis=-1, keepdims=True)
    i1 = jnp.min(jnp.where(pe == v1, lane, big), axis=-1, keepdims=True)
    pe2 = jnp.where((lane == i1) | ~in_grp, -1.0, pe)
    v2 = jnp.max(pe2, axis=-1, keepdims=True)
    i2 = jnp.min(jnp.where(pe2 == v2, lane, big), axis=-1, keepdims=True)
    tot = v1 + v2
    return jnp.where(lane == i1, g_w * (v1 / tot), jnp.where(lane == i2, g_w * (v2 / tot), 0.0))


def _for_rows(nrows, fn):
    n_grp = nrows // SUBLANES

    def group(i, c):
        for u in range(SUBLANES):
            fn(i * SUBLANES + u)
        return c
    lax.fori_loop(0, n_grp, group, 0)

    def single(r, c):
        fn(r)
        return c
    lax.fori_loop(n_grp * SUBLANES, nrows, single, 0)


def _moe_body(tg_ref, nv_ref, tok_ref, tok_next_ref, x_hbm, g_ref, wr_ref, br_ref, wg_ref, wu_ref, wd_ref,
              o_hbm, xbuf, obuf, gsem, ssem):
    i = pl.program_id(0)
    n = pl.num_programs(0)
    slot = i % 2

    def gathered(nrows, s):
        return pltpu.make_async_copy(x_hbm.at[pl.ds(0, nrows)], xbuf.at[s, pl.ds(0, nrows)], gsem.at[s])

    def scattered(nrows, s):
        return pltpu.make_async_copy(obuf.at[s, pl.ds(0, nrows)], o_hbm.at[pl.ds(0, nrows)], ssem.at[s])

    def wait_rows(rows, nrows, s):
        n8 = pl.multiple_of((nrows // SUBLANES) * SUBLANES, SUBLANES)

        @pl.when(n8 > 0)
        def _():
            rows(n8, s).wait()

        def body(r, c):
            rows(1, s).wait()
            return c
        lax.fori_loop(n8, nrows, body, 0)

    def gather(toks, nrows, s):
        def row(r):
            pltpu.make_async_copy(x_hbm.at[pl.ds(toks[0, r], 1)], xbuf.at[s, pl.ds(r, 1)], gsem.at[s]).start()
        _for_rows(nrows, row)

    @pl.when(i == 0)
    def _():
        xbuf[...] = jnp.zeros(xbuf.shape, F32)
        gather(tok_ref, nv_ref[0], 0)

    @pl.when(i + 1 < n)
    def _():
        gather(tok_next_ref, nv_ref[jnp.minimum(i + 1, n - 1)], 1 - slot)

    nv = nv_ref[i]
    nv_prev2 = nv_ref[jnp.maximum(i - 2, 0)]

    @pl.when(i >= 2)
    def _():
        wait_rows(scattered, nv_prev2, slot)

    @pl.when(nv > 0)
    def _():
        wait_rows(gathered, nv, slot)
        g = tg_ref[i]
        x = xbuf[slot]
        h = _rms(x, g_ref[...]).astype(BF16)
        gates = _gates_in_group(_dot(h, wr_ref[...]) + br_ref[...], g)
        lane = lax.broadcasted_iota(jnp.int32, gates.shape, 1)
        acc = x
        for j in range(EXPERTS_PER_GROUP):
            a = _dot(h, wg_ref[j])
            a = a * (1.0 / (1.0 + jnp.exp(-a))) * _dot(h, wu_ref[j])
            y = _dot(a.astype(BF16), wd_ref[j])
            gj = jnp.sum(jnp.where(lane == ROUTER_E0 + g * EXPERTS_PER_GROUP + j, gates, 0.0), axis=-1, keepdims=True)
            acc = acc + gj * y
        obuf[slot] = acc

        def row(r):
            pltpu.make_async_copy(obuf.at[slot, pl.ds(r, 1)], o_hbm.at[pl.ds(tok_ref[0, r], 1)], ssem.at[slot]).start()
        _for_rows(nv, row)

    @pl.when(i == n - 1)
    def _():
        @pl.when(i >= 1)
        def _():
            wait_rows(scattered, nv_ref[jnp.maximum(i - 1, 0)], 1 - slot)

        wait_rows(scattered, nv, slot)


def moe_sparse(x, grp, g, wr, br, wg, wu, wd, tm):
    T, D = x.shape
    Hd = wg.shape[-1]
    tile_grp, n_valid, tok = _dispatch_plan(grp, tm)
    n_tiles = tok.shape[0]
    const = lambda shape: pl.BlockSpec(shape, lambda i, tg, nv: (0,) * len(shape))
    by_group = lambda shape: pl.BlockSpec((None,) + shape, lambda i, tg, nv: (tg[i], 0, 0, 0))
    return pl.pallas_call(
        _moe_body,
        grid_spec=pltpu.PrefetchScalarGridSpec(
            num_scalar_prefetch=2,
            grid=(n_tiles,),
            in_specs=[pl.BlockSpec((None, 1, tm), lambda i, tg, nv: (i, 0, 0), memory_space=pltpu.SMEM),
                      pl.BlockSpec((None, 1, tm), lambda i, tg, nv: (jnp.minimum(i + 1, n_tiles - 1), 0, 0),
                                   memory_space=pltpu.SMEM),
                      pl.BlockSpec(memory_space=pl.ANY),
                      const((1, D)), const((D, LANES)), const((1, LANES)),
                      by_group((EXPERTS_PER_GROUP, D, Hd)), by_group((EXPERTS_PER_GROUP, D, Hd)),
                      by_group((EXPERTS_PER_GROUP, Hd, D))],
            out_specs=pl.BlockSpec(memory_space=pl.ANY),
            scratch_shapes=[pltpu.VMEM((2, tm, D), F32), pltpu.VMEM((2, tm, D), F32),
                            pltpu.SemaphoreType.DMA((2,)), pltpu.SemaphoreType.DMA((2,))]),
        out_shape=jax.ShapeDtypeStruct((T, D), F32),
        compiler_params=_cparams(("arbitrary",), disable_bounds_checks=True),
    )(tile_grp, n_valid, tok, tok, x, g.reshape(1, D), wr, br, wg, wu, wd)


def _final_norm_body(x_ref, g_ref, o_ref):
    o_ref[...] = _rms(x_ref[...], g_ref[...])


def final_norm(x, g, tm):
    T, D = x.shape
    return pl.pallas_call(
        _final_norm_body,
        grid=(T // tm,),
        in_specs=[pl.BlockSpec((tm, D), lambda i: (i, 0)), pl.BlockSpec((1, D), lambda i: (0, 0))],
        out_specs=pl.BlockSpec((tm, D), lambda i: (i, 0)),
        out_shape=jax.ShapeDtypeStruct((T, D), F32),
        compiler_params=_cparams(("parallel",)),
    )(x, g.reshape(1, D))


_C_PU = GLA_PACK
_C_DQ = _C_PU + POOL_WIDTH
_C_DK = _C_DQ + DIFF_QK
_C_DV = _C_DK + DIFF_QK
_C_END = _C_DV + DIFF_V
_IN_PLAN = (('rows', 0, GLA_PACK, F32), ('rows', _C_PU, _C_DQ, F32),
            ('heads', _C_DK, _C_DV, F32), ('heads', _C_DV, _C_END, F32),
            ('rows', _C_DQ, _C_DK, BF16), ('rows', _C_DK, _C_DV, BF16), ('rows', _C_DV, _C_END, BF16))


def _row_tile(T):
    return min(512, T)


def _layer(p, layer, lam_init, x, gla_s0, pool_hist, kv_past, kv_new, mk, mv):
    B, L, D = x.shape
    T = B * L
    tm = _row_tile(T)
    lam_vecs = (p['lambda_q1'], p['lambda_k1'], p['lambda_q2'], p['lambda_k2'])
    zg, pu, dk, dv, dq, dk16, dv16 = norm_proj(x.reshape(T, D), p['norm_mix_g'], p['w_in'], _IN_PLAN, tm,
                                               layer, kv_new)
    q3, k3, v3 = (t.reshape(B, L, DIFF_QK) for t in (dq, dk16, dv16))
    if kv_past is None:
        P = 0
        o_diff = diff_attention_prompt(q3, k3, v3, lam_vecs, p['diff_norm_g'], lam_init)
    else:
        P = kv_past[0].shape[2] // DIFF_HEADS
        o_diff = diff_attention_sample(q3, kv_past[0], kv_past[1], layer, k3, v3, lam_vecs, p['diff_norm_g'], lam_init)
    o_gla, gla_s = gla_mixer(zg.reshape(B, L, GLA_PACK), p['w_gla_a2'], p['b_gla_a'], p['gla_norm_g'], gla_s0)
    o_pool, pool_buf = pool_mixer(pu.reshape(B, L, POOL_WIDTH), pool_hist, P, p['pool_w'], p['pool_scale'])
    M = mk.shape[1]
    x3, grp = mix_mem(x, o_gla, o_pool, o_diff, p['w_out'], p['norm_mem_g'], p['w_mem_q'],
                      mk.reshape(B, M, MEM_HEADS * MEM_DH), mv.reshape(B, M, MEM_HEADS * MEM_DH), p['w_mem_o'],
                      p['norm_ffn_g'], p['w_router_group'], p['b_router_group'])
    x4 = moe_sparse(x3.reshape(T, D), grp, p['norm_ffn_g'], p['w_router'], p['b_router'],
                    p['w_exp_gate'], p['w_exp_up'], p['w_exp_down'], min(tm, T // 8))
    return x4.reshape(B, L, D), gla_s, pool_buf, (dk, dv)


def kernel(x_prompt, x_sample, mem_prompt, cache_diff_k, cache_diff_v, cache_mem_k, cache_mem_v, state_gla, state_pool, norm_mix_g, w_in, w_gla_a2, b_gla_a, gla_norm_g, pool_w, pool_scale, lambda_q1, lambda_k1, lambda_q2, lambda_k2, diff_norm_g, w_out, norm_mem_g, mem_norm_g, w_mem_q, w_mem_kv, w_mem_o, norm_ffn_g, w_router_group, b_router_group, w_router_expert, b_router_expert, w_exp_gate, w_exp_up, w_exp_down, final_norm_g):
    depth = w_in.shape[0]
    D = x_prompt.shape[-1]
    bp, mem_len = mem_prompt.shape[0], mem_prompt.shape[1]
    xp, xs = x_prompt, x_sample
    n_gla_in = 2 * GLA_QK + 2 * GLA_V + GLA_GATE_RANK
    mem_hd = MEM_HEADS * MEM_DH
    grouped = (N_GROUPS, EXPERTS_PER_GROUP)
    outs = [[] for _ in range(6)]
    bs, ls = x_sample.shape[0], x_sample.shape[1]
    lp = x_prompt.shape[1]
    past_len = cache_diff_k.shape[2]
    kv_p = tuple(jnp.zeros((depth, bp * lp * DIFF_HEADS, LANES), F32) for _ in range(2))
    kv_s = tuple(jnp.zeros((depth, bs * ls * DIFF_HEADS, LANES), F32) for _ in range(2))
    kv_cache = tuple(c.reshape(depth, bs, past_len * DIFF_HEADS, LANES) for c in (cache_diff_k, cache_diff_v))
    for l in range(depth):
        wi = w_in[l]
        w_in_p = jnp.concatenate([wi[:, :n_gla_in], jnp.zeros((D, GLA_PACK - n_gla_in), wi.dtype), wi[:, n_gla_in:]],
                                 axis=1).astype(BF16)
        pad = LANES - N_GROUPS - N_EXPERTS
        w_router = jnp.concatenate([w_router_group[l], w_router_expert[l], jnp.zeros((D, pad), F32)], axis=1).astype(BF16)
        b_router = jnp.concatenate([b_router_group[l], b_router_expert[l], jnp.zeros((pad,), F32)]).reshape(1, LANES)
        p = dict(norm_mix_g=norm_mix_g[l], w_in=w_in_p, w_gla_a2=w_gla_a2[l], b_gla_a=b_gla_a[l],
                 gla_norm_g=gla_norm_g[l], pool_w=pool_w[l], pool_scale=pool_scale[l],
                 lambda_q1=lambda_q1[l], lambda_k1=lambda_k1[l], lambda_q2=lambda_q2[l], lambda_k2=lambda_k2[l],
                 diff_norm_g=diff_norm_g[l], w_out=w_out[l].astype(BF16), norm_mem_g=norm_mem_g[l],
                 w_mem_q=w_mem_q[l].astype(BF16), w_mem_o=w_mem_o[l].astype(BF16), norm_ffn_g=norm_ffn_g[l],
                 w_router=w_router, b_router=b_router,
                 w_router_group=w_router_group[l], b_router_group=b_router_group[l],
                 w_exp_gate=w_exp_gate[l].astype(BF16).reshape(grouped + w_exp_gate.shape[2:]),
                 w_exp_up=w_exp_up[l].astype(BF16).reshape(grouped + w_exp_up.shape[2:]),
                 w_exp_down=w_exp_down[l].astype(BF16).reshape(grouped + w_exp_down.shape[2:]))
        lam_init = 0.8 - 0.6 * math.exp(-0.3 * l)
        mk, mv = norm_proj(mem_prompt.reshape(bp * mem_len, D), mem_norm_g[l], w_mem_kv[l].astype(BF16),
                           (('rows', 0, mem_hd, F32), ('rows', mem_hd, 2 * mem_hd, F32)), _row_tile(bp * mem_len))
        mk = mk.reshape(bp, mem_len, MEM_HEADS, MEM_DH)
        mv = mv.reshape(bp, mem_len, MEM_HEADS, MEM_DH)
        xp, g_new, pool_new, kv_p = _layer(
            p, l, lam_init, xp, jnp.zeros((bp, GLA_HEADS, GLA_DK, GLA_DV), F32),
            jnp.zeros((bp, POOL_HIST, POOL_WIDTH), F32), None, kv_p, mk, mv)
        for lst, t in zip(outs[:4], (mk, mv, g_new, pool_new)):
            lst.append(t)
        xs, g_new, pool_new, kv_s = _layer(
            p, l, lam_init, xs, state_gla[l], state_pool[l], kv_cache, kv_s, cache_mem_k[l], cache_mem_v[l])
        for lst, t in zip(outs[4:], (g_new, pool_new)):
            lst.append(t)
    y_prompt = final_norm(xp.reshape(-1, D), final_norm_g, _row_tile(xp.shape[0] * xp.shape[1])).reshape(xp.shape)
    y_sample = final_norm(xs.reshape(-1, D), final_norm_g, _row_tile(xs.shape[0] * xs.shape[1])).reshape(xs.shape)
    mk_p, mv_p, gs_p, ps_p, gs_s, ps_s = (jnp.stack(o) for o in outs)
    kv_shape_p = (depth, bp, lp, DIFF_HEADS, LANES)
    kv_shape_s = (depth, bs, ls, DIFF_HEADS, LANES)
    return (y_prompt, y_sample, kv_p[0].reshape(kv_shape_p), kv_p[1].reshape(kv_shape_p), mk_p, mv_p, gs_p, ps_p,
            kv_s[0].reshape(kv_shape_s), kv_s[1].reshape(kv_shape_s), gs_s, ps_s)
```

```python
import functools
import math

import jax
import jax.numpy as jnp
from jax import lax
from jax.experimental import pallas as pl
from jax.experimental.pallas import tpu as pltpu

F32 = jnp.float32
BF16 = jnp.bfloat16

EPS = 1e-6
CHUNK = 64
GLA_HEADS, GLA_DK, GLA_DV = 4, 32, 64
GLA_GATE_RANK = 16
GLA_TAU = 16.0
GLA_QK = GLA_HEADS * GLA_DK
GLA_V = GLA_HEADS * GLA_DV
POOL_WINDOWS = (2, 4, 8, 16)
POOL_GROUP = 64
POOL_WIDTH = 256
POOL_HIST = 15
POOL_CARRY = 16
DIFF_HEADS, DIFF_DH, DIFF_DV = 4, 64, 128
DIFF_QK = DIFF_HEADS * 2 * DIFF_DH
DIFF_V = DIFF_HEADS * DIFF_DV
MEM_HEADS, MEM_DH = 4, 128
N_GROUPS, EXPERTS_PER_GROUP, N_EXPERTS = 4, 4, 16
LANES = 128
SUBLANES = 8
GLA_PACK = 2 * GLA_QK + 2 * GLA_V + LANES
ROUTER_E0 = N_GROUPS
VMEM_LIMIT = 56 * 1024 * 1024


def _cparams(sem, **kw):
    return pltpu.CompilerParams(dimension_semantics=sem, vmem_limit_bytes=VMEM_LIMIT, **kw)


def _rms(x, g):
    ms = jnp.mean(x * x, axis=-1, keepdims=True)
    return x * lax.rsqrt(ms + EPS) * g


def _dot(a, b):
    return jnp.dot(a, b, preferred_element_type=F32)


def _dot_nt(a, b):
    return lax.dot_general(a, b, (((1,), (1,)), ((), ())), preferred_element_type=F32)


def _dot_tn(a, b):
    return lax.dot_general(a, b, (((0,), (0,)), ((), ())), preferred_element_type=F32)


N_PROJ_IN = 3


def _norm_proj_body(*refs, plan, n_stacked, tm):
    x_ref, g_ref, w_ref = refs[:N_PROJ_IN]
    o_refs = refs[N_PROJ_IN + n_stacked:]
    h = _rms(x_ref[...], g_ref[...]).astype(BF16)
    cols = {}
    for o_ref, (kind, a, b, dt) in zip(o_refs, plan):
        if (a, b) not in cols:
            cols[(a, b)] = _dot(h, w_ref[:, a:b])
        z = cols[(a, b)]
        if kind == 'rows':
            o_ref[...] = z.astype(dt)
        else:
            nh = (b - a) // LANES
            for hh in range(nh):
                o_ref[pl.ds(hh, tm, stride=nh), :] = z[:, hh * LANES:(hh + 1) * LANES].astype(dt)


def norm_proj(x, g, w, plan, tm, layer=0, stacked=()):
    T, D = x.shape
    out_shape, out_specs, aliases = [], [], {}
    stacked = list(stacked)
    n_stacked = 0
    for k, (kind, a, b, dt) in enumerate(plan):
        if kind == 'rows':
            out_shape.append(jax.ShapeDtypeStruct((T, b - a), dt))
            out_specs.append(pl.BlockSpec((tm, b - a), lambda i: (i, 0)))
        else:
            nh = (b - a) // LANES
            buf = stacked[n_stacked]
            out_shape.append(jax.ShapeDtypeStruct(buf.shape, buf.dtype))
            out_specs.append(pl.BlockSpec((None, tm * nh, LANES), lambda i: (layer, i, 0)))
            aliases[N_PROJ_IN + n_stacked] = k
            n_stacked += 1
    return pl.pallas_call(
        functools.partial(_norm_proj_body, plan=plan, n_stacked=n_stacked, tm=tm),
        grid=(T // tm,),
        in_specs=[pl.BlockSpec((tm, D), lambda i: (i, 0)),
                  pl.BlockSpec((1, D), lambda i: (0, 0)),
                  pl.BlockSpec(w.shape, lambda i: (0, 0))]
                 + [pl.BlockSpec(memory_space=pl.ANY)] * n_stacked,
        out_specs=out_specs, out_shape=out_shape,
        input_output_aliases=aliases,
        compiler_params=_cparams(("parallel",)),
    )(x, g.reshape(1, D), w, *stacked)


def _gla_body(zg_ref, wa_ref, ba_ref, gn_ref, s0_ref, o_ref, sout_ref, s_scr, *, n_chunks):
    C = CHUNK
    i = pl.program_id(1)

    @pl.when(i == 0)
    def _():
        s_scr[...] = s0_ref[...]

    rows = [slice(c * C, (c + 1) * C) for c in range(n_chunks)]
    tri = (lax.broadcasted_iota(jnp.int32, (C, C), 1) <= lax.broadcasted_iota(jnp.int32, (C, C), 0)).astype(F32)
    causal_h = (lax.broadcasted_iota(jnp.int32, (GLA_HEADS * C, C), 1)
                <= lax.broadcasted_iota(jnp.int32, (GLA_HEADS * C, C), 0) % C)
    head_k = lax.broadcasted_iota(jnp.int32, (C, GLA_QK), 1) // GLA_DK
    head_v = lax.broadcasted_iota(jnp.int32, (C, GLA_V), 1) // GLA_DV
    st_mask = (lax.broadcasted_iota(jnp.int32, (GLA_V, GLA_QK), 0) // GLA_DV
               == lax.broadcasted_iota(jnp.int32, (GLA_V, GLA_QK), 1) // GLA_DK)

    q = zg_ref[:, 0:GLA_QK] * (GLA_DK ** -0.5)
    k = zg_ref[:, GLA_QK:2 * GLA_QK]
    v = zg_ref[:, 2 * GLA_QK:2 * GLA_QK + GLA_V].astype(BF16)
    ga = zg_ref[:, 2 * GLA_QK + 2 * GLA_V:GLA_PACK].astype(BF16)
    x = _dot(ga, wa_ref[...]) + ba_ref[...]
    log_a = (jnp.minimum(x, 0.0) - jnp.log1p(jnp.exp(-jnp.abs(x)))) / GLA_TAU
    b = jnp.concatenate([jnp.dot(tri, log_a[r], preferred_element_type=F32, precision=lax.Precision.HIGHEST)
                         for r in rows], axis=0)

    def chunk_row(row):
        return jnp.concatenate([jnp.broadcast_to(b[c * C + row:c * C + row + 1], (C, GLA_QK))
                                for c in range(n_chunks)], axis=0)
    b_mid = chunk_row(C // 2 - 1)
    b_end = chunk_row(C - 1)
    q_in = q * jnp.exp(b - b_mid)
    k_in = (k * jnp.exp(b_mid - b)).astype(BF16)
    q_dec = (q * jnp.exp(b)).astype(BF16)
    k_dec = (k * jnp.exp(b_end - b)).astype(BF16)
    o_intra, kv = [], []
    for r in rows:
        q_st = jnp.concatenate([jnp.where(head_k == h, q_in[r], 0.0) for h in range(GLA_HEADS)], axis=0).astype(BF16)
        att = jnp.where(causal_h, _dot_nt(q_st, k_in[r]), 0.0).astype(BF16)
        res = _dot(att, v[r])
        o_c = jnp.where(head_v == 0, res[0:C], 0.0)
        for h in range(1, GLA_HEADS):
            o_c = o_c + jnp.where(head_v == h, res[h * C:(h + 1) * C], 0.0)
        o_intra.append(o_c)
        kv.append(jnp.where(st_mask, _dot_tn(v[r], k_dec[r]), 0.0))
    st = s_scr[...]
    o_parts = []
    for c, r in enumerate(rows):
        o_parts.append(o_intra[c] + _dot_nt(q_dec[r], st.astype(BF16)))
        st = st * jnp.exp(b[(c + 1) * C - 1:(c + 1) * C]) + kv[c]
    s_scr[...] = st
    o = jnp.concatenate(o_parts, axis=0)
    head_o = lax.broadcasted_iota(jnp.int32, o.shape, 1) // GLA_DV
    o2 = o * o
    ms = jnp.zeros_like(o)
    for h in range(GLA_HEADS):
        mh = jnp.sum(jnp.where(head_o == h, o2, 0.0), axis=-1, keepdims=True) * (1.0 / GLA_DV)
        ms = jnp.where(head_o == h, mh, ms)
    gr = zg_ref[:, 2 * GLA_QK + GLA_V:2 * GLA_QK + 2 * GLA_V]
    y = o * lax.rsqrt(ms + EPS) * gn_ref[...]
    o_ref[...] = (y * (gr * (1.0 / (1.0 + jnp.exp(-gr))))).astype(o_ref.dtype)

    @pl.when(i == pl.num_programs(1) - 1)
    def _():
        sout_ref[...] = st


def gla_mixer(zg, w_a2, b_a, norm_g, s0):
    B, L, _ = zg.shape
    tg = min(512, L)
    st0 = jnp.zeros((B, GLA_V, GLA_QK), F32)
    for h in range(GLA_HEADS):
        st0 = st0.at[:, h * GLA_DV:(h + 1) * GLA_DV, h * GLA_DK:(h + 1) * GLA_DK].set(
            jnp.swapaxes(s0[:, h].astype(F32), 1, 2))
    wa = jnp.zeros((LANES, GLA_QK), F32).at[:GLA_GATE_RANK].set(w_a2).astype(BF16)
    o, st = pl.pallas_call(
        functools.partial(_gla_body, n_chunks=tg // CHUNK),
        grid=(B, L // tg),
        in_specs=[pl.BlockSpec((None, tg, GLA_PACK), lambda b, i: (b, i, 0)),
                  pl.BlockSpec((LANES, GLA_QK), lambda b, i: (0, 0)),
                  pl.BlockSpec((1, GLA_QK), lambda b, i: (0, 0)),
                  pl.BlockSpec((1, GLA_V), lambda b, i: (0, 0)),
                  pl.BlockSpec((None, GLA_V, GLA_QK), lambda b, i: (b, 0, 0))],
        out_specs=[pl.BlockSpec((None, tg, GLA_V), lambda b, i: (b, i, 0)),
                   pl.BlockSpec((None, GLA_V, GLA_QK), lambda b, i: (b, 0, 0))],
        out_shape=[jax.ShapeDtypeStruct((B, L, GLA_V), BF16),
                   jax.ShapeDtypeStruct((B, GLA_V, GLA_QK), F32)],
        scratch_shapes=[pltpu.VMEM((GLA_V, GLA_QK), F32)],
        compiler_params=_cparams(("parallel", "arbitrary")),
    )(zg, wa, b_a.reshape(1, GLA_QK), jnp.tile(norm_g, GLA_HEADS).reshape(1, GLA_V), st0)
    s_fin = jnp.stack([jnp.swapaxes(st[:, h * GLA_DV:(h + 1) * GLA_DV, h * GLA_DK:(h + 1) * GLA_DK], 1, 2)
                       for h in range(GLA_HEADS)], axis=1)
    return o, s_fin


def _pool_body(u_ref, hist_ref, w_ref, sc_ref, o_ref, buf_ref, carry, *, pos0, tp):
    i = pl.program_id(1)

    @pl.when(i == 0)
    def _():
        carry[...] = hist_ref[...]

    u = u_ref[...]
    ext = jnp.concatenate([carry[...], u], axis=0)
    s2 = ext + pltpu.roll(ext, 1, 0)
    s4 = s2 + pltpu.roll(s2, 2, 0)
    s8 = s4 + pltpu.roll(s4, 4, 0)
    s16 = s8 + pltpu.roll(s8, 8, 0)
    grp = lax.broadcasted_iota(jnp.int32, (tp, POOL_WIDTH), 1) // POOL_GROUP
    pos = pos0 + i * tp + lax.broadcasted_iota(jnp.int32, (tp, POOL_WIDTH), 0)
    win = s16[POOL_CARRY:]
    width = jnp.full((tp, POOL_WIDTH), POOL_WINDOWS[3], jnp.int32)
    for gi, s in ((2, s8), (1, s4), (0, s2)):
        win = jnp.where(grp == gi, s[POOL_CARRY:], win)
        width = jnp.where(grp == gi, POOL_WINDOWS[gi], width)
    cnt = jnp.minimum(width, pos + 1).astype(F32)
    m = win / cnt - u
    o_ref[...] = (_dot(m.astype(BF16), w_ref[...]) * sc_ref[...]).astype(o_ref.dtype)
    tail = ext[tp:]
    carry[...] = tail

    @pl.when(i == pl.num_programs(1) - 1)
    def _():
        buf_ref[...] = tail


def pool_mixer(u, hist, pos0, w_pool, scale):
    B, L, W = u.shape
    tp = min(512, L)
    hist_p = jnp.pad(hist.astype(F32), ((0, 0), (POOL_CARRY - POOL_HIST, 0), (0, 0)))
    wbd = jnp.zeros((W, W), F32)
    for gi in range(len(POOL_WINDOWS)):
        lo = gi * POOL_GROUP
        wbd = wbd.at[lo:lo + POOL_GROUP, lo:lo + POOL_GROUP].set(w_pool[gi])
    o, buf = pl.pallas_call(
        functools.partial(_pool_body, pos0=pos0, tp=tp),
        grid=(B, L // tp),
        in_specs=[pl.BlockSpec((None, tp, W), lambda b, i: (b, i, 0)),
                  pl.BlockSpec((None, POOL_CARRY, W), lambda b, i: (b, 0, 0)),
                  pl.BlockSpec((W, W), lambda b, i: (0, 0)),
                  pl.BlockSpec((1, W), lambda b, i: (0, 0))],
        out_specs=[pl.BlockSpec((None, tp, W), lambda b, i: (b, i, 0)),
                   pl.BlockSpec((None, POOL_CARRY, W), lambda b, i: (b, 0, 0))],
        out_shape=[jax.ShapeDtypeStruct((B, L, W), BF16),
                   jax.ShapeDtypeStruct((B, POOL_CARRY, W), F32)],
        scratch_shapes=[pltpu.VMEM((POOL_CARRY, W), F32)],
        compiler_params=_cparams(("parallel", "arbitrary")),
    )(u, hist_p, wbd.astype(BF16), scale.reshape(1, W))
    return o, buf[:, POOL_CARRY - POOL_HIST:]


def _lam(lq1, lk1, lq2, lk2, lam_init):
    return (jnp.exp(jnp.sum(lq1[...] * lk1[...], axis=-1, keepdims=True))
            - jnp.exp(jnp.sum(lq2[...] * lk2[...], axis=-1, keepdims=True)) + lam_init)


def _half_masks(n):
    lane = lax.broadcasted_iota(jnp.int32, (n, 2 * DIFF_DH), 1)
    return lane < DIFF_DH, lane >= DIFF_DH


DIFF_Q_SCALE = DIFF_DH ** -0.5 * math.log2(math.e)


DIFF_PAIR = 2


def _flash_update(scores, values, m_ref, l_ref, acc_ref):
    n = len(scores)
    tk = scores[0].shape[1]
    m_prev = [m_ref[i] for i in range(n)]
    m_new = [jnp.maximum(m_prev[i], jnp.max(scores[i], axis=-1, keepdims=True)) for i in range(n)]
    alpha = [jnp.exp2(m_prev[i] - m_new[i]) for i in range(n)]
    m_cols = [jnp.concatenate([m] * (tk // LANES), axis=1) if tk % LANES == 0 else m[:, :tk] for m in m_new]
    p = [jnp.exp2(scores[i] - m_cols[i]) for i in range(n)]
    for i in range(n):
        l_ref[i] = alpha[i] * l_ref[i] + jnp.sum(p[i], axis=-1, keepdims=True)
    pv = [_dot(p[i].astype(BF16), values[i]) for i in range(n)]
    for i in range(n):
        acc_ref[i] = alpha[i] * acc_ref[i] + pv[i]
        m_ref[i] = m_new[i]


def _diff_finish(acc_ref, l_ref, i0, i1, lam, g, post_scale):
    o = acc_ref[i0] / l_ref[i0] - lam * (acc_ref[i1] / l_ref[i1])
    return _rms(o, g) * post_scale


def _diff_prompt_body(q_ref, k_ref, v_ref, lq1, lk1, lq2, lk2, g_ref, o_ref, m_ref, l_ref, acc_ref,
                      *, tq, lam_init):
    qi = pl.program_id(1)
    lam = _lam(lq1, lk1, lq2, lk2, lam_init)
    keep = _half_masks(tq)
    r_c = lax.broadcasted_iota(jnp.int32, (tq, tq), 0) // CHUNK
    c_c = lax.broadcasted_iota(jnp.int32, (tq, tq), 1) // CHUNK
    visible = c_c <= r_c
    for h0 in range(0, DIFF_HEADS, DIFF_PAIR):
        heads = range(h0, h0 + DIFF_PAIR)
        cols = [slice(h * DIFF_DV, (h + 1) * DIFF_DV) for h in heads]
        q_m = []
        for c in cols:
            q_h = q_ref[:, c].astype(F32) * DIFF_Q_SCALE
            q_m += [jnp.where(keep[mm], q_h, 0.0).astype(BF16) for mm in range(2)]
        m_ref[...] = jnp.full(m_ref.shape, -jnp.inf, F32)
        l_ref[...] = jnp.zeros(l_ref.shape, F32)
        acc_ref[...] = jnp.zeros(acc_ref.shape, F32)

        def block(kb, masked):
            ks = pl.ds(pl.multiple_of(kb * tq, tq), tq)
            k_h = [k_ref[ks, c] for c in cols]
            v_h = [v_ref[ks, c] for c in cols]
            s = [_dot_nt(q_m[i], k_h[i // 2]) for i in range(2 * DIFF_PAIR)]
            if masked:
                s = [jnp.where(visible, t, -jnp.inf) for t in s]
            _flash_update(s, [v_h[i // 2] for i in range(2 * DIFF_PAIR)], m_ref, l_ref, acc_ref)

        def body(kb, carry):
            block(kb, False)
            return carry

        lax.fori_loop(0, qi, body, 0)
        block(qi, True)
        for a, c in enumerate(cols):
            y = _diff_finish(acc_ref, l_ref, 2 * a, 2 * a + 1, lam, g_ref[...], 1.0 - lam_init)
            o_ref[:, c] = y.astype(o_ref.dtype)


def diff_attention_prompt(q, k, v, lam_vecs, norm_g, lam_init):
    B, L, _ = q.shape
    tq = min(512, L)
    vec = pl.BlockSpec((1, DIFF_DH), lambda b, i: (0, 0))
    return pl.pallas_call(
        functools.partial(_diff_prompt_body, tq=tq, lam_init=lam_init),
        grid=(B, L // tq),
        in_specs=[pl.BlockSpec((None, tq, DIFF_QK), lambda b, i: (b, i, 0)),
                  pl.BlockSpec((None, L, DIFF_QK), lambda b, i: (b, 0, 0)),
                  pl.BlockSpec((None, L, DIFF_V), lambda b, i: (b, 0, 0)),
                  vec, vec, vec, vec,
                  pl.BlockSpec((1, DIFF_DV), lambda b, i: (0, 0))],
        out_specs=pl.BlockSpec((None, tq, DIFF_V), lambda b, i: (b, i, 0)),
        out_shape=jax.ShapeDtypeStruct((B, L, DIFF_V), BF16),
        scratch_shapes=[pltpu.VMEM((2 * DIFF_PAIR, tq, LANES), F32), pltpu.VMEM((2 * DIFF_PAIR, tq, LANES), F32),
                        pltpu.VMEM((2 * DIFF_PAIR, tq, DIFF_DV), F32)],
        compiler_params=_cparams(("parallel", "arbitrary")),
    )(q, k, v, *[t.reshape(1, DIFF_DH) for t in lam_vecs], norm_g.reshape(1, DIFF_DV))


def _diff_sample_body(q_ref, kp_ref, vp_ref, kn_ref, vn_ref, lq1, lk1, lq2, lk2, g_ref, o_ref,
                      m_ref, l_ref, acc_ref, *, lam_init):
    j = pl.program_id(1)
    last = pl.num_programs(1) - 1
    lq = q_ref.shape[0]
    keep = _half_masks(lq)

    @pl.when(j == 0)
    def _():
        m_ref[...] = jnp.full(m_ref.shape, -jnp.inf, F32)
        l_ref[...] = jnp.zeros(l_ref.shape, F32)
        acc_ref[...] = jnp.zeros(acc_ref.shape, F32)

    def step(head_rows):
        scores, values = [], []
        for h in range(DIFF_HEADS):
            q_h = q_ref[:, h * DIFF_DV:(h + 1) * DIFF_DV].astype(F32) * DIFF_Q_SCALE
            k_h, v_h = head_rows(h)
            for mm in range(2):
                q_m = jnp.where(keep[mm], q_h, 0.0).astype(BF16)
                scores.append(_dot_nt(q_m, k_h))
                values.append(v_h)
        _flash_update(scores, values, m_ref, l_ref, acc_ref)

    def cached(h):
        rows = pl.ds(h, kp_ref.shape[0] // DIFF_HEADS, stride=DIFF_HEADS)
        return kp_ref[rows, :].astype(BF16), vp_ref[rows, :].astype(BF16)

    def fresh(h):
        cols = slice(h * DIFF_DV, (h + 1) * DIFF_DV)
        return kn_ref[:, cols], vn_ref[:, cols]

    @pl.when(j < last)
    def _():
        step(cached)

    @pl.when(j == last)
    def _():
        step(fresh)
        lam = _lam(lq1, lk1, lq2, lk2, lam_init)
        for h in range(DIFF_HEADS):
            y = _diff_finish(acc_ref, l_ref, 2 * h, 2 * h + 1, lam, g_ref[...], 1.0 - lam_init)
            o_ref[:, h * DIFF_DV:(h + 1) * DIFF_DV] = y.astype(o_ref.dtype)


def diff_attention_sample(q, k_past, v_past, layer, k_new, v_new, lam_vecs, norm_g, lam_init):
    B, lq, _ = q.shape
    P = k_past.shape[2] // DIFF_HEADS
    tk = min(1024, P)
    n_past = P // tk
    vec = pl.BlockSpec((1, DIFF_DH), lambda b, j: (0, 0))
    past = pl.BlockSpec((None, None, tk * DIFF_HEADS, LANES),
                        lambda b, j: (layer, b, jnp.minimum(j, n_past - 1), 0))
    new = pl.BlockSpec((None, lq, DIFF_QK), lambda b, j: (b, 0, 0))
    n_state = 2 * DIFF_HEADS
    return pl.pallas_call(
        functools.partial(_diff_sample_body, lam_init=lam_init),
        grid=(B, n_past + 1),
        in_specs=[new, past, past, new, new, vec, vec, vec, vec,
                  pl.BlockSpec((1, DIFF_DV), lambda b, j: (0, 0))],
        out_specs=pl.BlockSpec((None, lq, DIFF_V), lambda b, j: (b, 0, 0)),
        out_shape=jax.ShapeDtypeStruct((B, lq, DIFF_V), BF16),
        scratch_shapes=[pltpu.VMEM((n_state, lq, LANES), F32), pltpu.VMEM((n_state, lq, LANES), F32),
                        pltpu.VMEM((n_state, lq, DIFF_DV), F32)],
        compiler_params=_cparams(("parallel", "arbitrary")),
    )(q, k_past, v_past, k_new, v_new, *[t.reshape(1, DIFF_DH) for t in lam_vecs], norm_g.reshape(1, DIFF_DV))


ROUTER_ROWS = 8


def _mix_mem_body(x_ref, og_ref, op_ref, od_ref, wout_ref, gm_ref, wq_ref, mk_ref, mv_ref, wo_ref,
                  gf_ref, wt_ref, bt_ref, o_ref, grp_ref):
    y = _dot(og_ref[...], wout_ref[0:GLA_V, :])
    y = y + _dot(op_ref[...], wout_ref[GLA_V:GLA_V + POOL_WIDTH, :])
    y = y + _dot(od_ref[...], wout_ref[GLA_V + POOL_WIDTH:, :])
    x = x_ref[...] + y
    h = _rms(x, gm_ref[...]).astype(BF16)
    q = _dot(h, wq_ref[...]).astype(BF16)
    outs = []
    for hd in range(MEM_HEADS):
        cols = slice(hd * MEM_DH, (hd + 1) * MEM_DH)
        s = _dot_nt(q[:, cols], mk_ref[:, cols].astype(BF16)) * (MEM_DH ** -0.5)
        e = jnp.exp(s - jnp.max(s, axis=-1, keepdims=True))
        p = (e / jnp.sum(e, axis=-1, keepdims=True)).astype(BF16)
        outs.append(_dot(p, mv_ref[:, cols].astype(BF16)).astype(BF16))
    x = x + _dot(jnp.concatenate(outs, axis=-1), wo_ref[...])
    o_ref[...] = x
    hf = _rms(x, gf_ref[...]).astype(BF16)
    lt = _dot_nt(wt_ref[...], hf) + bt_ref[...]
    row = lax.broadcasted_iota(jnp.int32, lt.shape, 0)
    lg = jnp.where(row < N_GROUPS, lt, -jnp.inf)
    g_max = jnp.max(lg, axis=0, keepdims=True)
    grp_ref[...] = jnp.min(jnp.where(lg == g_max, row, ROUTER_ROWS), axis=0, keepdims=True)


def mix_mem(x, og, op, od, w_out, g_mem, wq, mk, mv, wo, g_ffn, w_group, b_group):
    B, L, D = x.shape
    M, HD = mk.shape[1], mk.shape[2]
    tm = min(512, L)
    n_l = L // tm
    wt = jnp.zeros((ROUTER_ROWS, D), F32).at[:N_GROUPS].set(w_group.T).astype(BF16)
    bt = jnp.zeros((ROUTER_ROWS, 1), F32).at[:N_GROUPS, 0].set(b_group)
    const = lambda shape: pl.BlockSpec(shape, lambda b, i: (0,) * len(shape))
    rows = lambda n: pl.BlockSpec((None, tm, n), lambda b, i: (b, i, 0))
    per_b = pl.BlockSpec((None, M, HD), lambda b, i: (b, 0, 0))
    x3, grp = pl.pallas_call(
        _mix_mem_body,
        grid=(B, n_l),
        in_specs=[rows(D), rows(GLA_V), rows(POOL_WIDTH), rows(DIFF_V), const(w_out.shape),
                  const((1, D)), const(wq.shape), per_b, per_b, const(wo.shape),
                  const((1, D)), const((ROUTER_ROWS, D)), const((ROUTER_ROWS, 1))],
        out_specs=[rows(D), pl.BlockSpec((None, 1, tm), lambda b, i: (b * n_l + i, 0, 0))],
        out_shape=[jax.ShapeDtypeStruct((B, L, D), F32), jax.ShapeDtypeStruct((B * n_l, 1, tm), jnp.int32)],
        compiler_params=_cparams(("parallel", "parallel")),
    )(x, og, op, od, w_out, g_mem.reshape(1, D), wq, mk, mv, wo, g_ffn.reshape(1, D), wt, bt)
    return x3, grp.reshape(B * L)


def _dispatch_plan(grp, tm):
    T = grp.shape[0]
    n_tiles = T // tm + N_GROUPS
    tok_sorted = jnp.sort(grp * T + jnp.arange(T, dtype=jnp.int32)) % T
    gids = jnp.arange(N_GROUPS, dtype=jnp.int32)
    counts = jnp.sum((grp[None, :] == gids[:, None]).astype(jnp.int32), axis=1)
    tiles = (counts + tm - 1) // tm
    tile_end = jnp.cumsum(tiles)
    tile_start = tile_end - tiles
    row_start = jnp.cumsum(counts) - counts
    i = jnp.arange(n_tiles, dtype=jnp.int32)
    tile_grp = jnp.minimum(jnp.sum((i[:, None] >= tile_end[None, :]).astype(jnp.int32), axis=1), N_GROUPS - 1)
    local = i - tile_start[tile_grp]
    n_valid = jnp.clip(counts[tile_grp] - local * tm, 0, tm)
    idx = jnp.clip((row_start[tile_grp] + local * tm)[:, None] + jnp.arange(tm, dtype=jnp.int32)[None, :], 0, T - 1)
    return tile_grp.astype(jnp.int32), n_valid.astype(jnp.int32), tok_sorted[idx].reshape(n_tiles, 1, tm)


def _gates_in_group(logits, g):
    lane = lax.broadcasted_iota(jnp.int32, logits.shape, 1)
    big = jnp.int32(LANES)
    neg = -jnp.inf
    lg = jnp.where(lane < N_GROUPS, logits, neg)
    g_max = jnp.max(lg, axis=-1, keepdims=True)
    l_g = jnp.sum(jnp.where(lane == g, logits, 0.0), axis=-1, keepdims=True)
    g_w = jnp.exp(l_g - g_max) / jnp.sum(jnp.exp(lg - g_max), axis=-1, keepdims=True)
    e0 = ROUTER_E0 + g * EXPERTS_PER_GROUP
    in_grp = (lane >= e0) & (lane < e0 + EXPERTS_PER_GROUP)
    le = jnp.where(in_grp, logits, neg)
    e_max = jnp.max(le, axis=-1, keepdims=True)
    pe = jnp.exp(le - e_max)
    pe = pe / jnp.sum(pe, axis=-1, keepdims=True)
    v1 = jnp.max(pe, axis=-1, keepdims=True)
    i1 = jnp.min(jnp.where(pe == v1, lane, big), axis=-1, keepdims=True)
    pe2 = jnp.where((lane == i1) | ~in_grp, -1.0, pe)
    v2 = jnp.max(pe2, axis=-1, keepdims=True)
    i2 = jnp.min(jnp.where(pe2 == v2, lane, big), axis=-1, keepdims=True)
    tot = v1 + v2
    return jnp.where(lane == i1, g_w * (v1 / tot), jnp.where(lane == i2, g_w * (v2 / tot), 0.0))


def _for_rows(nrows, fn):
    n_grp = nrows // SUBLANES

    def group(i, c):
        for u in range(SUBLANES):
            fn(i * SUBLANES + u)
        return c
    lax.fori_loop(0, n_grp, group, 0)

    def single(r, c):
        fn(r)
        return c
    lax.fori_loop(n_grp * SUBLANES, nrows, single, 0)


def _moe_body(tg_ref, nv_ref, tok_ref, tok_next_ref, x_hbm, g_ref, wr_ref, br_ref, wg_ref, wu_ref, wd_ref,
              o_hbm, xbuf, obuf, gsem, ssem):
    i = pl.program_id(0)
    n = pl.num_programs(0)
    slot = i % 2

    def gathered(nrows, s):
        return pltpu.make_async_copy(x_hbm.at[pl.ds(0, nrows)], xbuf.at[s, pl.ds(0, nrows)], gsem.at[s])

    def scattered(nrows, s):
        return pltpu.make_async_copy(obuf.at[s, pl.ds(0, nrows)], o_hbm.at[pl.ds(0, nrows)], ssem.at[s])

    def wait_rows(rows, nrows, s):
        n8 = pl.multiple_of((nrows // SUBLANES) * SUBLANES, SUBLANES)

        @pl.when(n8 > 0)
        def _():
            rows(n8, s).wait()

        def body(r, c):
            rows(1, s).wait()
            return c
        lax.fori_loop(n8, nrows, body, 0)

    def gather(toks, nrows, s):
        def row(r):
            pltpu.make_async_copy(x_hbm.at[pl.ds(toks[0, r], 1)], xbuf.at[s, pl.ds(r, 1)], gsem.at[s]).start()
        _for_rows(nrows, row)

    @pl.when(i == 0)
    def _():
        xbuf[...] = jnp.zeros(xbuf.shape, F32)
        gather(tok_ref, nv_ref[0], 0)

    @pl.when(i + 1 < n)
    def _():
        gather(tok_next_ref, nv_ref[jnp.minimum(i + 1, n - 1)], 1 - slot)

    nv = nv_ref[i]
    nv_prev2 = nv_ref[jnp.maximum(i - 2, 0)]

    @pl.when(i >= 2)
    def _():
        wait_rows(scattered, nv_prev2, slot)

    @pl.when(nv > 0)
    def _():
        wait_rows(gathered, nv, slot)
        g = tg_ref[i]
        x = xbuf[slot]
        h = _rms(x, g_ref[...]).astype(BF16)
        gates = _gates_in_group(_dot(h, wr_ref[...]) + br_ref[...], g)
        lane = lax.broadcasted_iota(jnp.int32, gates.shape, 1)
        acc = x
        for j in range(EXPERTS_PER_GROUP):
            a = _dot(h, wg_ref[j])
            a = a * (1.0 / (1.0 + jnp.exp(-a))) * _dot(h, wu_ref[j])
            y = _dot(a.astype(BF16), wd_ref[j])
            gj = jnp.sum(jnp.where(lane == ROUTER_E0 + g * EXPERTS_PER_GROUP + j, gates, 0.0), axis=-1, keepdims=True)
            acc = acc + gj * y
        obuf[slot] = acc

        def row(r):
            pltpu.make_async_copy(obuf.at[slot, pl.ds(r, 1)], o_hbm.at[pl.ds(tok_ref[0, r], 1)], ssem.at[slot]).start()
        _for_rows(nv, row)

    @pl.when(i == n - 1)
    def _():
        @pl.when(i >= 1)
        def _():
            wait_rows(scattered, nv_ref[jnp.maximum(i - 1, 0)], 1 - slot)

        wait_rows(scattered, nv, slot)


def moe_sparse(x, grp, g, wr, br, wg, wu, wd, tm):
    T, D = x.shape
    Hd = wg.shape[-1]
    tile_grp, n_valid, tok = _dispatch_plan(grp, tm)
    n_tiles = tok.shape[0]
    const = lambda shape: pl.BlockSpec(shape, lambda i, tg, nv: (0,) * len(shape))
    by_group = lambda shape: pl.BlockSpec((None,) + shape, lambda i, tg, nv: (tg[i], 0, 0, 0))
    return pl.pallas_call(
        _moe_body,
        grid_spec=pltpu.PrefetchScalarGridSpec(
            num_scalar_prefetch=2,
            grid=(n_tiles,),
            in_specs=[pl.BlockSpec((None, 1, tm), lambda i, tg, nv: (i, 0, 0), memory_space=pltpu.SMEM),
                      pl.BlockSpec((None, 1, tm), lambda i, tg, nv: (jnp.minimum(i + 1, n_tiles - 1), 0, 0),
                                   memory_space=pltpu.SMEM),
                      pl.BlockSpec(memory_space=pl.ANY),
                      const((1, D)), const((D, LANES)), const((1, LANES)),
                      by_group((EXPERTS_PER_GROUP, D, Hd)), by_group((EXPERTS_PER_GROUP, D, Hd)),
                      by_group((EXPERTS_PER_GROUP, Hd, D))],
            out_specs=pl.BlockSpec(memory_space=pl.ANY),
            scratch_shapes=[pltpu.VMEM((2, tm, D), F32), pltpu.VMEM((2, tm, D), F32),
                            pltpu.SemaphoreType.DMA((2,)), pltpu.SemaphoreType.DMA((2,))]),
        out_shape=jax.ShapeDtypeStruct((T, D), F32),
        compiler_params=_cparams(("arbitrary",), disable_bounds_checks=True),
    )(tile_grp, n_valid, tok, tok, x, g.reshape(1, D), wr, br, wg, wu, wd)


def _final_norm_body(x_ref, g_ref, o_ref):
    o_ref[...] = _rms(x_ref[...], g_ref[...])


def final_norm(x, g, tm):
    T, D = x.shape
    return pl.pallas_call(
        _final_norm_body,
        grid=(T // tm,),
        in_specs=[pl.BlockSpec((tm, D), lambda i: (i, 0)), pl.BlockSpec((1, D), lambda i: (0, 0))],
        out_specs=pl.BlockSpec((tm, D), lambda i: (i, 0)),
        out_shape=jax.ShapeDtypeStruct((T, D), F32),
        compiler_params=_cparams(("parallel",)),
    )(x, g.reshape(1, D))


_C_PU = GLA_PACK
_C_DQ = _C_PU + POOL_WIDTH
_C_DK = _C_DQ + DIFF_QK
_C_DV = _C_DK + DIFF_QK
_C_END = _C_DV + DIFF_V
_IN_PLAN = (('rows', 0, GLA_PACK, F32), ('rows', _C_PU, _C_DQ, F32),
            ('heads', _C_DK, _C_DV, F32), ('heads', _C_DV, _C_END, F32),
            ('rows', _C_DQ, _C_DK, BF16), ('rows', _C_DK, _C_DV, BF16), ('rows', _C_DV, _C_END, BF16))


def _row_tile(T):
    return min(512, T)


def _layer(p, layer, lam_init, x, gla_s0, pool_hist, kv_past, kv_new, mk, mv):
    B, L, D = x.shape
    T = B * L
    tm = _row_tile(T)
    lam_vecs = (p['lambda_q1'], p['lambda_k1'], p['lambda_q2'], p['lambda_k2'])
    zg, pu, dk, dv, dq, dk16, dv16 = norm_proj(x.reshape(T, D), p['norm_mix_g'], p['w_in'], _IN_PLAN, tm,
                                               layer, kv_new)
    q3, k3, v3 = (t.reshape(B, L, DIFF_QK) for t in (dq, dk16, dv16))
    if kv_past is None:
        P = 0
        o_diff = diff_attention_prompt(q3, k3, v3, lam_vecs, p['diff_norm_g'], lam_init)
    else:
        P = kv_past[0].shape[2] // DIFF_HEADS
        o_diff = diff_attention_sample(q3, kv_past[0], kv_past[1], layer, k3, v3, lam_vecs, p['diff_norm_g'], lam_init)
    o_gla, gla_s = gla_mixer(zg.reshape(B, L, GLA_PACK), p['w_gla_a2'], p['b_gla_a'], p['gla_norm_g'], gla_s0)
    o_pool, pool_buf = pool_mixer(pu.reshape(B, L, POOL_WIDTH), pool_hist, P, p['pool_w'], p['pool_scale'])
    M = mk.shape[1]
    x3, grp = mix_mem(x, o_gla, o_pool, o_diff, p['w_out'], p['norm_mem_g'], p['w_mem_q'],
                      mk.reshape(B, M, MEM_HEADS * MEM_DH), mv.reshape(B, M, MEM_HEADS * MEM_DH), p['w_mem_o'],
                      p['norm_ffn_g'], p['w_router_group'], p['b_router_group'])
    x4 = moe_sparse(x3.reshape(T, D), grp, p['norm_ffn_g'], p['w_router'], p['b_router'],
                    p['w_exp_gate'], p['w_exp_up'], p['w_exp_down'], min(tm, T // 8))
    return x4.reshape(B, L, D), gla_s, pool_buf, (dk, dv)


def kernel(x_prompt, x_sample, mem_prompt, cache_diff_k, cache_diff_v, cache_mem_k, cache_mem_v, state_gla, state_pool, norm_mix_g, w_in, w_gla_a2, b_gla_a, gla_norm_g, pool_w, pool_scale, lambda_q1, lambda_k1, lambda_q2, lambda_k2, diff_norm_g, w_out, norm_mem_g, mem_norm_g, w_mem_q, w_mem_kv, w_mem_o, norm_ffn_g, w_router_group, b_router_group, w_router_expert, b_router_expert, w_exp_gate, w_exp_up, w_exp_down, final_norm_g):
    depth = w_in.shape[0]
    D = x_prompt.shape[-1]
    bp, mem_len = mem_prompt.shape[0], mem_prompt.shape[1]
    xp, xs = x_prompt, x_sample
    n_gla_in = 2 * GLA_QK + 2 * GLA_V + GLA_GATE_RANK
    mem_hd = MEM_HEADS * MEM_DH
    grouped = (N_GROUPS, EXPERTS_PER_GROUP)
    outs = [[] for _ in range(6)]
    bs, ls = x_sample.shape[0], x_sample.shape[1]
    lp = x_prompt.shape[1]
    past_len = cache_diff_k.shape[2]
    kv_p = tuple(jnp.zeros((depth, bp * lp * DIFF_HEADS, LANES), F32) for _ in range(2))
    kv_s = tuple(jnp.zeros((depth, bs * ls * DIFF_HEADS, LANES), F32) for _ in range(2))
    kv_cache = tuple(c.reshape(depth, bs, past_len * DIFF_HEADS, LANES) for c in (cache_diff_k, cache_diff_v))
    for l in range(depth):
        wi = w_in[l]
        w_in_p = jnp.concatenate([wi[:, :n_gla_in], jnp.zeros((D, GLA_PACK - n_gla_in), wi.dtype), wi[:, n_gla_in:]],
                                 axis=1).astype(BF16)
        pad = LANES - N_GROUPS - N_EXPERTS
        w_router = jnp.concatenate([w_router_group[l], w_router_expert[l], jnp.zeros((D, pad), F32)], axis=1).astype(BF16)
        b_router = jnp.concatenate([b_router_group[l], b_router_expert[l], jnp.zeros((pad,), F32)]).reshape(1, LANES)
        p = dict(norm_mix_g=norm_mix_g[l], w_in=w_in_p, w_gla_a2=w_gla_a2[l], b_gla_a=b_gla_a[l],
                 gla_norm_g=gla_norm_g[l], pool_w=pool_w[l], pool_scale=pool_scale[l],
                 lambda_q1=lambda_q1[l], lambda_k1=lambda_k1[l], lambda_q2=lambda_q2[l], lambda_k2=lambda_k2[l],
                 diff_norm_g=diff_norm_g[l], w_out=w_out[l].astype(BF16), norm_mem_g=norm_mem_g[l],
                 w_mem_q=w_mem_q[l].astype(BF16), w_mem_o=w_mem_o[l].astype(BF16), norm_ffn_g=norm_ffn_g[l],
                 w_router=w_router, b_router=b_router,
                 w_router_group=w_router_group[l], b_router_group=b_router_group[l],
                 w_exp_gate=w_exp_gate[l].astype(BF16).reshape(grouped + w_exp_gate.shape[2:]),
                 w_exp_up=w_exp_up[l].astype(BF16).reshape(grouped + w_exp_up.shape[2:]),
                 w_exp_down=w_exp_down[l].astype(BF16).reshape(grouped + w_exp_down.shape[2:]))
        lam_init = 0.8 - 0.6 * math.exp(-0.3 * l)
        mk, mv = norm_proj(mem_prompt.reshape(bp * mem_len, D), mem_norm_g[l], w_mem_kv[l].astype(BF16),
                           (('rows', 0, mem_hd, F32), ('rows', mem_hd, 2 * mem_hd, F32)), _row_tile(bp * mem_len))
        mk = mk.reshape(bp, mem_len, MEM_HEADS, MEM_DH)
        mv = mv.reshape(bp, mem_len, MEM_HEADS, MEM_DH)
        xp, g_new, pool_new, kv_p = _layer(
            p, l, lam_init, xp, jnp.zeros((bp, GLA_HEADS, GLA_DK, GLA_DV), F32),
            jnp.zeros((bp, POOL_HIST, POOL_WIDTH), F32), None, kv_p, mk, mv)
        for lst, t in zip(outs[:4], (mk, mv, g_new, pool_new)):
            lst.append(t)
        xs, g_new, pool_new, kv_s = _layer(
            p, l, lam_init, xs, state_gla[l], state_pool[l], kv_cache, kv_s, cache_mem_k[l], cache_mem_v[l])
        for lst, t in zip(outs[4:], (g_new, pool_new)):
            lst.append(t)
    y_prompt = final_norm(xp.reshape(-1, D), final_norm_g, _row_tile(xp.shape[0] * xp.shape[1])).reshape(xp.shape)
    y_sample = final_norm(xs.reshape(-1, D), final_norm_g, _row_tile(xs.shape[0] * xs.shape[1])).reshape(xs.shape)
    mk_p, mv_p, gs_p, ps_p, gs_s, ps_s = (jnp.stack(o) for o in outs)
    kv_shape_p = (depth, bp, lp, DIFF_HEADS, LANES)
    kv_shape_s = (depth, bs, ls, DIFF_HEADS, LANES)
    return (y_prompt, y_sample, kv_p[0].reshape(kv_shape_p), kv_p[1].reshape(kv_shape_p), mk_p, mv_p, gs_p, ps_p,
            kv_s[0].reshape(kv_shape_s), kv_s[1].reshape(kv_shape_s), gs_s, ps_s)
```

```python
import functools
import math

import jax
import jax.numpy as jnp
from jax import lax
from jax.experimental import pallas as pl
from jax.experimental.pallas import tpu as pltpu

F32 = jnp.float32
BF16 = jnp.bfloat16

EPS = 1e-6
CHUNK = 64
GLA_HEADS, GLA_DK, GLA_DV = 4, 32, 64
GLA_GATE_RANK = 16
GLA_TAU = 16.0
GLA_QK = GLA_HEADS * GLA_DK
GLA_V = GLA_HEADS * GLA_DV
POOL_WINDOWS = (2, 4, 8, 16)
POOL_GROUP = 64
POOL_WIDTH = 256
POOL_HIST = 15
POOL_CARRY = 16
DIFF_HEADS, DIFF_DH, DIFF_DV = 4, 64, 128
DIFF_QK = DIFF_HEADS * 2 * DIFF_DH
DIFF_V = DIFF_HEADS * DIFF_DV
MEM_HEADS, MEM_DH = 4, 128
N_GROUPS, EXPERTS_PER_GROUP, N_EXPERTS = 4, 4, 16
LANES = 128
SUBLANES = 8
GLA_PACK = 2 * GLA_QK + 2 * GLA_V + LANES
ROUTER_E0 = N_GROUPS
VMEM_LIMIT = 56 * 1024 * 1024


def _cparams(sem, **kw):
    return pltpu.CompilerParams(dimension_semantics=sem, vmem_limit_bytes=VMEM_LIMIT, **kw)


def _rms(x, g):
    ms = jnp.mean(x * x, axis=-1, keepdims=True)
    return x * lax.rsqrt(ms + EPS) * g


def _dot(a, b):
    return jnp.dot(a, b, preferred_element_type=F32)


def _dot_nt(a, b):
    return lax.dot_general(a, b, (((1,), (1,)), ((), ())), preferred_element_type=F32)


def _dot_tn(a, b):
    return lax.dot_general(a, b, (((0,), (0,)), ((), ())), preferred_element_type=F32)


N_PROJ_IN = 3


def _norm_proj_body(*refs, plan, n_stacked, tm):
    x_ref, g_ref, w_ref = refs[:N_PROJ_IN]
    o_refs = refs[N_PROJ_IN + n_stacked:]
    h = _rms(x_ref[...], g_ref[...]).astype(BF16)
    cols = {}
    for o_ref, (kind, a, b, dt) in zip(o_refs, plan):
        if (a, b) not in cols:
            cols[(a, b)] = _dot(h, w_ref[:, a:b])
        z = cols[(a, b)]
        if kind == 'rows':
            o_ref[...] = z.astype(dt)
        else:
            nh = (b - a) // LANES
            for hh in range(nh):
                o_ref[pl.ds(hh, tm, stride=nh), :] = z[:, hh * LANES:(hh + 1) * LANES].astype(dt)


def norm_proj(x, g, w, plan, tm, layer=0, stacked=()):
    T, D = x.shape
    out_shape, out_specs, aliases = [], [], {}
    stacked = list(stacked)
    n_stacked = 0
    for k, (kind, a, b, dt) in enumerate(plan):
        if kind == 'rows':
            out_shape.append(jax.ShapeDtypeStruct((T, b - a), dt))
            out_specs.append(pl.BlockSpec((tm, b - a), lambda i: (i, 0)))
        else:
            nh = (b - a) // LANES
            buf = stacked[n_stacked]
            out_shape.append(jax.ShapeDtypeStruct(buf.shape, buf.dtype))
            out_specs.append(pl.BlockSpec((None, tm * nh, LANES), lambda i: (layer, i, 0)))
            aliases[N_PROJ_IN + n_stacked] = k
            n_stacked += 1
    return pl.pallas_call(
        functools.partial(_norm_proj_body, plan=plan, n_stacked=n_stacked, tm=tm),
        grid=(T // tm,),
        in_specs=[pl.BlockSpec((tm, D), lambda i: (i, 0)),
                  pl.BlockSpec((1, D), lambda i: (0, 0)),
                  pl.BlockSpec(w.shape, lambda i: (0, 0))]
                 + [pl.BlockSpec(memory_space=pl.ANY)] * n_stacked,
        out_specs=out_specs, out_shape=out_shape,
        input_output_aliases=aliases,
        compiler_params=_cparams(("parallel",)),
    )(x, g.reshape(1, D), w, *stacked)


def _gla_body(zg_ref, wa_ref, ba_ref, gn_ref, s0_ref, o_ref, sout_ref, s_scr, *, n_chunks):
    C = CHUNK
    i = pl.program_id(1)

    @pl.when(i == 0)
    def _():
        s_scr[...] = s0_ref[...]

    rows = [slice(c * C, (c + 1) * C) for c in range(n_chunks)]
    tri = (lax.broadcasted_iota(jnp.int32, (C, C), 1) <= lax.broadcasted_iota(jnp.int32, (C, C), 0)).astype(F32)
    causal_h = (lax.broadcasted_iota(jnp.int32, (GLA_HEADS * C, C), 1)
                <= lax.broadcasted_iota(jnp.int32, (GLA_HEADS * C, C), 0) % C)
    head_k = lax.broadcasted_iota(jnp.int32, (C, GLA_QK), 1) // GLA_DK
    head_v = lax.broadcasted_iota(jnp.int32, (C, GLA_V), 1) // GLA_DV
    st_mask = (lax.broadcasted_iota(jnp.int32, (GLA_V, GLA_QK), 0) // GLA_DV
               == lax.broadcasted_iota(jnp.int32, (GLA_V, GLA_QK), 1) // GLA_DK)

    q = zg_ref[:, 0:GLA_QK] * (GLA_DK ** -0.5)
    k = zg_ref[:, GLA_QK:2 * GLA_QK]
    v = zg_ref[:, 2 * GLA_QK:2 * GLA_QK + GLA_V].astype(BF16)
    ga = zg_ref[:, 2 * GLA_QK + 2 * GLA_V:GLA_PACK].astype(BF16)
    x = _dot(ga, wa_ref[...]) + ba_ref[...]
    log_a = (jnp.minimum(x, 0.0) - jnp.log1p(jnp.exp(-jnp.abs(x)))) / GLA_TAU
    b = jnp.concatenate([jnp.dot(tri, log_a[r], preferred_element_type=F32, precision=lax.Precision.HIGHEST)
                         for r in rows], axis=0)

    def chunk_row(row):
        return jnp.concatenate([jnp.broadcast_to(b[c * C + row:c * C + row + 1], (C, GLA_QK))
                                for c in range(n_chunks)], axis=0)
    b_mid = chunk_row(C // 2 - 1)
    b_end = chunk_row(C - 1)
    q_in = q * jnp.exp(b - b_mid)
    k_in = (k * jnp.exp(b_mid - b)).astype(BF16)
    q_dec = (q * jnp.exp(b)).astype(BF16)
    k_dec = (k * jnp.exp(b_end - b)).astype(BF16)
    o_intra, kv = [], []
    for r in rows:
        q_st = jnp.concatenate([jnp.where(head_k == h, q_in[r], 0.0) for h in range(GLA_HEADS)], axis=0).astype(BF16)
        att = jnp.where(causal_h, _dot_nt(q_st, k_in[r]), 0.0).astype(BF16)
        res = _dot(att, v[r])
        o_c = jnp.where(head_v == 0, res[0:C], 0.0)
        for h in range(1, GLA_HEADS):
            o_c = o_c + jnp.where(head_v == h, res[h * C:(h + 1) * C], 0.0)
        o_intra.append(o_c)
        kv.append(jnp.where(st_mask, _dot_tn(v[r], k_dec[r]), 0.0))
    st = s_scr[...]
    o_parts = []
    for c, r in enumerate(rows):
        o_parts.append(o_intra[c] + _dot_nt(q_dec[r], st.astype(BF16)))
        st = st * jnp.exp(b[(c + 1) * C - 1:(c + 1) * C]) + kv[c]
    s_scr[...] = st
    o = jnp.concatenate(o_parts, axis=0)
    head_o = lax.broadcasted_iota(jnp.int32, o.shape, 1) // GLA_DV
    o2 = o * o
    ms = jnp.zeros_like(o)
    for h in range(GLA_HEADS):
        mh = jnp.sum(jnp.where(head_o == h, o2, 0.0), axis=-1, keepdims=True) * (1.0 / GLA_DV)
        ms = jnp.where(head_o == h, mh, ms)
    gr = zg_ref[:, 2 * GLA_QK + GLA_V:2 * GLA_QK + 2 * GLA_V]
    y = o * lax.rsqrt(ms + EPS) * gn_ref[...]
    o_ref[...] = (y * (gr * (1.0 / (1.0 + jnp.exp(-gr))))).astype(o_ref.dtype)

    @pl.when(i == pl.num_programs(1) - 1)
    def _():
        sout_ref[...] = st


def gla_mixer(zg, w_a2, b_a, norm_g, s0):
    B, L, _ = zg.shape
    tg = min(512, L)
    st0 = jnp.zeros((B, GLA_V, GLA_QK), F32)
    for h in range(GLA_HEADS):
        st0 = st0.at[:, h * GLA_DV:(h + 1) * GLA_DV, h * GLA_DK:(h + 1) * GLA_DK].set(
            jnp.swapaxes(s0[:, h].astype(F32), 1, 2))
    wa = jnp.zeros((LANES, GLA_QK), F32).at[:GLA_GATE_RANK].set(w_a2).astype(BF16)
    o, st = pl.pallas_call(
        functools.partial(_gla_body, n_chunks=tg // CHUNK),
        grid=(B, L // tg),
        in_specs=[pl.BlockSpec((None, tg, GLA_PACK), lambda b, i: (b, i, 0)),
                  pl.BlockSpec((LANES, GLA_QK), lambda b, i: (0, 0)),
                  pl.BlockSpec((1, GLA_QK), lambda b, i: (0, 0)),
                  pl.BlockSpec((1, GLA_V), lambda b, i: (0, 0)),
                  pl.BlockSpec((None, GLA_V, GLA_QK), lambda b, i: (b, 0, 0))],
        out_specs=[pl.BlockSpec((None, tg, GLA_V), lambda b, i: (b, i, 0)),
                   pl.BlockSpec((None, GLA_V, GLA_QK), lambda b, i: (b, 0, 0))],
        out_shape=[jax.ShapeDtypeStruct((B, L, GLA_V), BF16),
                   jax.ShapeDtypeStruct((B, GLA_V, GLA_QK), F32)],
        scratch_shapes=[pltpu.VMEM((GLA_V, GLA_QK), F32)],
        compiler_params=_cparams(("parallel", "arbitrary")),
    )(zg, wa, b_a.reshape(1, GLA_QK), jnp.tile(norm_g, GLA_HEADS).reshape(1, GLA_V), st0)
    s_fin = jnp.stack([jnp.swapaxes(st[:, h * GLA_DV:(h + 1) * GLA_DV, h * GLA_DK:(h + 1) * GLA_DK], 1, 2)
                       for h in range(GLA_HEADS)], axis=1)
    return o, s_fin


def _pool_body(u_ref, hist_ref, w_ref, sc_ref, o_ref, buf_ref, carry, *, pos0, tp):
    i = pl.program_id(1)

    @pl.when(i == 0)
    def _():
        carry[...] = hist_ref[...]

    u = u_ref[...]
    ext = jnp.concatenate([carry[...], u], axis=0)
    s2 = ext + pltpu.roll(ext, 1, 0)
    s4 = s2 + pltpu.roll(s2, 2, 0)
    s8 = s4 + pltpu.roll(s4, 4, 0)
    s16 = s8 + pltpu.roll(s8, 8, 0)
    grp = lax.broadcasted_iota(jnp.int32, (tp, POOL_WIDTH), 1) // POOL_GROUP
    pos = pos0 + i * tp + lax.broadcasted_iota(jnp.int32, (tp, POOL_WIDTH), 0)
    win = s16[POOL_CARRY:]
    width = jnp.full((tp, POOL_WIDTH), POOL_WINDOWS[3], jnp.int32)
    for gi, s in ((2, s8), (1, s4), (0, s2)):
        win = jnp.where(grp == gi, s[POOL_CARRY:], win)
        width = jnp.where(grp == gi, POOL_WINDOWS[gi], width)
    cnt = jnp.minimum(width, pos + 1).astype(F32)
    m = win / cnt - u
    o_ref[...] = (_dot(m.astype(BF16), w_ref[...]) * sc_ref[...]).astype(o_ref.dtype)
    tail = ext[tp:]
    carry[...] = tail

    @pl.when(i == pl.num_programs(1) - 1)
    def _():
        buf_ref[...] = tail


def pool_mixer(u, hist, pos0, w_pool, scale):
    B, L, W = u.shape
    tp = min(512, L)
    hist_p = jnp.pad(hist.astype(F32), ((0, 0), (POOL_CARRY - POOL_HIST, 0), (0, 0)))
    wbd = jnp.zeros((W, W), F32)
    for gi in range(len(POOL_WINDOWS)):
        lo = gi * POOL_GROUP
        wbd = wbd.at[lo:lo + POOL_GROUP, lo:lo + POOL_GROUP].set(w_pool[gi])
    o, buf = pl.pallas_call(
        functools.partial(_pool_body, pos0=pos0, tp=tp),
        grid=(B, L // tp),
        in_specs=[pl.BlockSpec((None, tp, W), lambda b, i: (b, i, 0)),
                  pl.BlockSpec((None, POOL_CARRY, W), lambda b, i: (b, 0, 0)),
                  pl.BlockSpec((W, W), lambda b, i: (0, 0)),
                  pl.BlockSpec((1, W), lambda b, i: (0, 0))],
        out_specs=[pl.BlockSpec((None, tp, W), lambda b, i: (b, i, 0)),
                   pl.BlockSpec((None, POOL_CARRY, W), lambda b, i: (b, 0, 0))],
        out_shape=[jax.ShapeDtypeStruct((B, L, W), BF16),
                   jax.ShapeDtypeStruct((B, POOL_CARRY, W), F32)],
        scratch_shapes=[pltpu.VMEM((POOL_CARRY, W), F32)],
        compiler_params=_cparams(("parallel", "arbitrary")),
    )(u, hist_p, wbd.astype(BF16), scale.reshape(1, W))
    return o, buf[:, POOL_CARRY - POOL_HIST:]


def _lam(lq1, lk1, lq2, lk2, lam_init):
    return (jnp.exp(jnp.sum(lq1[...] * lk1[...], axis=-1, keepdims=True))
            - jnp.exp(jnp.sum(lq2[...] * lk2[...], axis=-1, keepdims=True)) + lam_init)


def _half_masks(n):
    lane = lax.broadcasted_iota(jnp.int32, (n, 2 * DIFF_DH), 1)
    return lane < DIFF_DH, lane >= DIFF_DH


DIFF_Q_SCALE = DIFF_DH ** -0.5 * math.log2(math.e)


DIFF_PAIR = 2


def _flash_update(scores, values, m_ref, l_ref, acc_ref):
    n = len(scores)
    tk = scores[0].shape[1]
    m_prev = [m_ref[i] for i in range(n)]
    m_new = [jnp.maximum(m_prev[i], jnp.max(scores[i], axis=-1, keepdims=True)) for i in range(n)]
    alpha = [jnp.exp2(m_prev[i] - m_new[i]) for i in range(n)]
    m_cols = [jnp.concatenate([m] * (tk // LANES), axis=1) if tk % LANES == 0 else m[:, :tk] for m in m_new]
    p = [jnp.exp2(scores[i] - m_cols[i]) for i in range(n)]
    for i in range(n):
        l_ref[i] = alpha[i] * l_ref[i] + jnp.sum(p[i], axis=-1, keepdims=True)
    pv = [_dot(p[i].astype(BF16), values[i]) for i in range(n)]
    for i in range(n):
        acc_ref[i] = alpha[i] * acc_ref[i] + pv[i]
        m_ref[i] = m_new[i]


def _diff_finish(acc_ref, l_ref, i0, i1, lam, g, post_scale):
    o = acc_ref[i0] / l_ref[i0] - lam * (acc_ref[i1] / l_ref[i1])
    return _rms(o, g) * post_scale


def _diff_prompt_body(q_ref, k_ref, v_ref, lq1, lk1, lq2, lk2, g_ref, o_ref, m_ref, l_ref, acc_ref,
                      *, tq, lam_init):
    qi = pl.program_id(1)
    lam = _lam(lq1, lk1, lq2, lk2, lam_init)
    keep = _half_masks(tq)
    r_c = lax.broadcasted_iota(jnp.int32, (tq, tq), 0) // CHUNK
    c_c = lax.broadcasted_iota(jnp.int32, (tq, tq), 1) // CHUNK
    visible = c_c <= r_c
    wide = 2 * tq
    for h0 in range(0, DIFF_HEADS, DIFF_PAIR):
        heads = range(h0, h0 + DIFF_PAIR)
        cols = [slice(h * DIFF_DV, (h + 1) * DIFF_DV) for h in heads]
        q_m = []
        for c in cols:
            q_h = q_ref[:, c].astype(F32) * DIFF_Q_SCALE
            q_m += [jnp.where(keep[mm], q_h, 0.0).astype(BF16) for mm in range(2)]
        m_ref[...] = jnp.full(m_ref.shape, -jnp.inf, F32)
        l_ref[...] = jnp.zeros(l_ref.shape, F32)
        acc_ref[...] = jnp.zeros(acc_ref.shape, F32)

        def block(start, width, mask):
            ks = pl.ds(pl.multiple_of(start, tq), width)
            k_h = [k_ref[ks, c] for c in cols]
            v_h = [v_ref[ks, c] for c in cols]
            s = [_dot_nt(q_m[i], k_h[i // 2]) for i in range(2 * DIFF_PAIR)]
            if mask is not None:
                s = [jnp.where(mask, t, -jnp.inf) for t in s]
            _flash_update(s, [v_h[i // 2] for i in range(2 * DIFF_PAIR)], m_ref, l_ref, acc_ref)

        def body(kb, carry):
            block(kb * wide, wide, None)
            return carry

        lax.fori_loop(0, qi // 2, body, 0)

        @pl.when(qi % 2 == 1)
        def _():
            k_c = lax.broadcasted_iota(jnp.int32, (tq, wide), 1) // CHUNK - tq // CHUNK
            q_c = lax.broadcasted_iota(jnp.int32, (tq, wide), 0) // CHUNK
            block((qi - 1) * tq, wide, k_c <= q_c)

        @pl.when(qi % 2 == 0)
        def _():
            block(qi * tq, tq, visible)

        for a, c in enumerate(cols):
            y = _diff_finish(acc_ref, l_ref, 2 * a, 2 * a + 1, lam, g_ref[...], 1.0 - lam_init)
            o_ref[:, c] = y.astype(o_ref.dtype)


def diff_attention_prompt(q, k, v, lam_vecs, norm_g, lam_init):
    B, L, _ = q.shape
    tq = min(512, L)
    vec = pl.BlockSpec((1, DIFF_DH), lambda b, i: (0, 0))
    return pl.pallas_call(
        functools.partial(_diff_prompt_body, tq=tq, lam_init=lam_init),
        grid=(B, L // tq),
        in_specs=[pl.BlockSpec((None, tq, DIFF_QK), lambda b, i: (b, i, 0)),
                  pl.BlockSpec((None, L, DIFF_QK), lambda b, i: (b, 0, 0)),
                  pl.BlockSpec((None, L, DIFF_V), lambda b, i: (b, 0, 0)),
                  vec, vec, vec, vec,
                  pl.BlockSpec((1, DIFF_DV), lambda b, i: (0, 0))],
        out_specs=pl.BlockSpec((None, tq, DIFF_V), lambda b, i: (b, i, 0)),
        out_shape=jax.ShapeDtypeStruct((B, L, DIFF_V), BF16),
        scratch_shapes=[pltpu.VMEM((2 * DIFF_PAIR, tq, LANES), F32), pltpu.VMEM((2 * DIFF_PAIR, tq, LANES), F32),
                        pltpu.VMEM((2 * DIFF_PAIR, tq, DIFF_DV), F32)],
        compiler_params=_cparams(("parallel", "arbitrary")),
    )(q, k, v, *[t.reshape(1, DIFF_DH) for t in lam_vecs], norm_g.reshape(1, DIFF_DV))


def _diff_sample_body(q_ref, kp_ref, vp_ref, kn_ref, vn_ref, lq1, lk1, lq2, lk2, g_ref, o_ref,
                      m_ref, l_ref, acc_ref, *, lam_init):
    j = pl.program_id(1)
    last = pl.num_programs(1) - 1
    lq = q_ref.shape[0]
    keep = _half_masks(lq)

    @pl.when(j == 0)
    def _():
        m_ref[...] = jnp.full(m_ref.shape, -jnp.inf, F32)
        l_ref[...] = jnp.zeros(l_ref.shape, F32)
        acc_ref[...] = jnp.zeros(acc_ref.shape, F32)

    def step(head_rows):
        scores, values = [], []
        for h in range(DIFF_HEADS):
            q_h = q_ref[:, h * DIFF_DV:(h + 1) * DIFF_DV].astype(F32) * DIFF_Q_SCALE
            k_h, v_h = head_rows(h)
            for mm in range(2):
                q_m = jnp.where(keep[mm], q_h, 0.0).astype(BF16)
                scores.append(_dot_nt(q_m, k_h))
                values.append(v_h)
        _flash_update(scores, values, m_ref, l_ref, acc_ref)

    def cached(h):
        rows = pl.ds(h, kp_ref.shape[0] // DIFF_HEADS, stride=DIFF_HEADS)
        return kp_ref[rows, :].astype(BF16), vp_ref[rows, :].astype(BF16)

    def fresh(h):
        cols = slice(h * DIFF_DV, (h + 1) * DIFF_DV)
        return kn_ref[:, cols], vn_ref[:, cols]

    @pl.when(j < last)
    def _():
        step(cached)

    @pl.when(j == last)
    def _():
        step(fresh)
        lam = _lam(lq1, lk1, lq2, lk2, lam_init)
        for h in range(DIFF_HEADS):
            y = _diff_finish(acc_ref, l_ref, 2 * h, 2 * h + 1, lam, g_ref[...], 1.0 - lam_init)
            o_ref[:, h * DIFF_DV:(h + 1) * DIFF_DV] = y.astype(o_ref.dtype)


def diff_attention_sample(q, k_past, v_past, layer, k_new, v_new, lam_vecs, norm_g, lam_init):
    B, lq, _ = q.shape
    P = k_past.shape[2] // DIFF_HEADS
    tk = min(2048, P)
    n_past = P // tk
    vec = pl.BlockSpec((1, DIFF_DH), lambda b, j: (0, 0))
    past = pl.BlockSpec((None, None, tk * DIFF_HEADS, LANES),
                        lambda b, j: (layer, b, jnp.minimum(j, n_past - 1), 0))
    new = pl.BlockSpec((None, lq, DIFF_QK), lambda b, j: (b, 0, 0))
    n_state = 2 * DIFF_HEADS
    return pl.pallas_call(
        functools.partial(_diff_sample_body, lam_init=lam_init),
        grid=(B, n_past + 1),
        in_specs=[new, past, past, new, new, vec, vec, vec, vec,
                  pl.BlockSpec((1, DIFF_DV), lambda b, j: (0, 0))],
        out_specs=pl.BlockSpec((None, lq, DIFF_V), lambda b, j: (b, 0, 0)),
        out_shape=jax.ShapeDtypeStruct((B, lq, DIFF_V), BF16),
        scratch_shapes=[pltpu.VMEM((n_state, lq, LANES), F32), pltpu.VMEM((n_state, lq, LANES), F32),
                        pltpu.VMEM((n_state, lq, DIFF_DV), F32)],
        compiler_params=_cparams(("parallel", "arbitrary")),
    )(q, k_past, v_past, k_new, v_new, *[t.reshape(1, DIFF_DH) for t in lam_vecs], norm_g.reshape(1, DIFF_DV))


ROUTER_ROWS = 8


def _mix_mem_body(x_ref, og_ref, op_ref, od_ref, wout_ref, gm_ref, wq_ref, mk_ref, mv_ref, wo_ref,
                  gf_ref, wt_ref, bt_ref, o_ref, grp_ref):
    y = _dot(og_ref[...], wout_ref[0:GLA_V, :])
    y = y + _dot(op_ref[...], wout_ref[GLA_V:GLA_V + POOL_WIDTH, :])
    y = y + _dot(od_ref[...], wout_ref[GLA_V + POOL_WIDTH:, :])
    x = x_ref[...] + y
    h = _rms(x, gm_ref[...]).astype(BF16)
    q = _dot(h, wq_ref[...]).astype(BF16)
    outs = []
    for hd in range(MEM_HEADS):
        cols = slice(hd * MEM_DH, (hd + 1) * MEM_DH)
        s = _dot_nt(q[:, cols], mk_ref[:, cols].astype(BF16)) * (MEM_DH ** -0.5)
        e = jnp.exp(s - jnp.max(s, axis=-1, keepdims=True))
        p = (e / jnp.sum(e, axis=-1, keepdims=True)).astype(BF16)
        outs.append(_dot(p, mv_ref[:, cols].astype(BF16)).astype(BF16))
    x = x + _dot(jnp.concatenate(outs, axis=-1), wo_ref[...])
    o_ref[...] = x
    hf = _rms(x, gf_ref[...]).astype(BF16)
    lt = _dot_nt(wt_ref[...], hf) + bt_ref[...]
    row = lax.broadcasted_iota(jnp.int32, lt.shape, 0)
    lg = jnp.where(row < N_GROUPS, lt, -jnp.inf)
    g_max = jnp.max(lg, axis=0, keepdims=True)
    grp_ref[...] = jnp.min(jnp.where(lg == g_max, row, ROUTER_ROWS), axis=0, keepdims=True)


def mix_mem(x, og, op, od, w_out, g_mem, wq, mk, mv, wo, g_ffn, w_group, b_group):
    B, L, D = x.shape
    M, HD = mk.shape[1], mk.shape[2]
    tm = min(512, L)
    n_l = L // tm
    wt = jnp.zeros((ROUTER_ROWS, D), F32).at[:N_GROUPS].set(w_group.T).astype(BF16)
    bt = jnp.zeros((ROUTER_ROWS, 1), F32).at[:N_GROUPS, 0].set(b_group)
    const = lambda shape: pl.BlockSpec(shape, lambda b, i: (0,) * len(shape))
    rows = lambda n: pl.BlockSpec((None, tm, n), lambda b, i: (b, i, 0))
    per_b = pl.BlockSpec((None, M, HD), lambda b, i: (b, 0, 0))
    x3, grp = pl.pallas_call(
        _mix_mem_body,
        grid=(B, n_l),
        in_specs=[rows(D), rows(GLA_V), rows(POOL_WIDTH), rows(DIFF_V), const(w_out.shape),
                  const((1, D)), const(wq.shape), per_b, per_b, const(wo.shape),
                  const((1, D)), const((ROUTER_ROWS, D)), const((ROUTER_ROWS, 1))],
        out_specs=[rows(D), pl.BlockSpec((None, 1, tm), lambda b, i: (b * n_l + i, 0, 0))],
        out_shape=[jax.ShapeDtypeStruct((B, L, D), F32), jax.ShapeDtypeStruct((B * n_l, 1, tm), jnp.int32)],
        compiler_params=_cparams(("parallel", "parallel")),
    )(x, og, op, od, w_out, g_mem.reshape(1, D), wq, mk, mv, wo, g_ffn.reshape(1, D), wt, bt)
    return x3, grp.reshape(B * L)


def _dispatch_plan(grp, tm):
    T = grp.shape[0]
    n_tiles = T // tm + N_GROUPS
    tok_sorted = jnp.sort(grp * T + jnp.arange(T, dtype=jnp.int32)) % T
    gids = jnp.arange(N_GROUPS, dtype=jnp.int32)
    counts = jnp.sum((grp[None, :] == gids[:, None]).astype(jnp.int32), axis=1)
    tiles = (counts + tm - 1) // tm
    tile_end = jnp.cumsum(tiles)
    tile_start = tile_end - tiles
    row_start = jnp.cumsum(counts) - counts
    i = jnp.arange(n_tiles, dtype=jnp.int32)
    tile_grp = jnp.minimum(jnp.sum((i[:, None] >= tile_end[None, :]).astype(jnp.int32), axis=1), N_GROUPS - 1)
    local = i - tile_start[tile_grp]
    n_valid = jnp.clip(counts[tile_grp] - local * tm, 0, tm)
    idx = jnp.clip((row_start[tile_grp] + local * tm)[:, None] + jnp.arange(tm, dtype=jnp.int32)[None, :], 0, T - 1)
    return tile_grp.astype(jnp.int32), n_valid.astype(jnp.int32), tok_sorted[idx].reshape(n_tiles, 1, tm)


def _gates_in_group(logits, g):
    lane = lax.broadcasted_iota(jnp.int32, logits.shape, 1)
    big = jnp.int32(LANES)
    neg = -jnp.inf
    lg = jnp.where(lane < N_GROUPS, logits, neg)
    g_max = jnp.max(lg, axis=-1, keepdims=True)
    l_g = jnp.sum(jnp.where(lane == g, logits, 0.0), axis=-1, keepdims=True)
    g_w = jnp.exp(l_g - g_max) / jnp.sum(jnp.exp(lg - g_max), axis=-1, keepdims=True)
    e0 = ROUTER_E0 + g * EXPERTS_PER_GROUP
    in_grp = (lane >= e0) & (lane < e0 + EXPERTS_PER_GROUP)
    le = jnp.where(in_grp, logits, neg)
    e_max = jnp.max(le, axis=-1, keepdims=True)
    pe = jnp.exp(le - e_max)
    pe = pe / jnp.sum(pe, axis=-1, keepdims=True)
    v1 = jnp.max(pe, axis=-1, keepdims=True)
    i1 = jnp.min(jnp.where(pe == v1, lane, big), axis=-1, keepdims=True)
    pe2 = jnp.where((lane == i1) | ~in_grp, -1.0, pe)
    v2 = jnp.max(pe2, axis=-1, keepdims=True)
    i2 = jnp.min(jnp.where(pe2 == v2, lane, big), axis=-1, keepdims=True)
    tot = v1 + v2
    return jnp.where(lane == i1, g_w * (v1 / tot), jnp.where(lane == i2, g_w * (v2 / tot), 0.0))


def _for_rows(nrows, fn):
    n_grp = nrows // SUBLANES

    def group(i, c):
        for u in range(SUBLANES):
            fn(i * SUBLANES + u)
        return c
    lax.fori_loop(0, n_grp, group, 0)

    def single(r, c):
        fn(r)
        return c
    lax.fori_loop(n_grp * SUBLANES, nrows, single, 0)


def _moe_body(tg_ref, nv_ref, tok_ref, tok_next_ref, x_hbm, g_ref, wr_ref, br_ref, wg_ref, wu_ref, wd_ref,
              o_hbm, xbuf, obuf, gsem, ssem):
    i = pl.program_id(0)
    n = pl.num_programs(0)
    slot = i % 2

    def gathered(nrows, s):
        return pltpu.make_async_copy(x_hbm.at[pl.ds(0, nrows)], xbuf.at[s, pl.ds(0, nrows)], gsem.at[s])

    def scattered(nrows, s):
        return pltpu.make_async_copy(obuf.at[s, pl.ds(0, nrows)], o_hbm.at[pl.ds(0, nrows)], ssem.at[s])

    def wait_rows(rows, nrows, s):
        n8 = pl.multiple_of((nrows // SUBLANES) * SUBLANES, SUBLANES)

        @pl.when(n8 > 0)
        def _():
            rows(n8, s).wait()

        def body(r, c):
            rows(1, s).wait()
            return c
        lax.fori_loop(n8, nrows, body, 0)

    def gather(toks, nrows, s):
        def row(r):
            pltpu.make_async_copy(x_hbm.at[pl.ds(toks[0, r], 1)], xbuf.at[s, pl.ds(r, 1)], gsem.at[s]).start()
        _for_rows(nrows, row)

    @pl.when(i == 0)
    def _():
        xbuf[...] = jnp.zeros(xbuf.shape, F32)
        gather(tok_ref, nv_ref[0], 0)

    @pl.when(i + 1 < n)
    def _():
        gather(tok_next_ref, nv_ref[jnp.minimum(i + 1, n - 1)], 1 - slot)

    nv = nv_ref[i]
    nv_prev2 = nv_ref[jnp.maximum(i - 2, 0)]

    @pl.when(i >= 2)
    def _():
        wait_rows(scattered, nv_prev2, slot)

    @pl.when(nv > 0)
    def _():
        wait_rows(gathered, nv, slot)
        g = tg_ref[i]
        x = xbuf[slot]
        h = _rms(x, g_ref[...]).astype(BF16)
        gates = _gates_in_group(_dot(h, wr_ref[...]) + br_ref[...], g)
        lane = lax.broadcasted_iota(jnp.int32, gates.shape, 1)
        acc = x
        for j in range(EXPERTS_PER_GROUP):
            a = _dot(h, wg_ref[j])
            a = a * (1.0 / (1.0 + jnp.exp(-a))) * _dot(h, wu_ref[j])
            y = _dot(a.astype(BF16), wd_ref[j])
            gj = jnp.sum(jnp.where(lane == ROUTER_E0 + g * EXPERTS_PER_GROUP + j, gates, 0.0), axis=-1, keepdims=True)
            acc = acc + gj * y
        obuf[slot] = acc

        def row(r):
            pltpu.make_async_copy(obuf.at[slot, pl.ds(r, 1)], o_hbm.at[pl.ds(tok_ref[0, r], 1)], ssem.at[slot]).start()
        _for_rows(nv, row)

    @pl.when(i == n - 1)
    def _():
        @pl.when(i >= 1)
        def _():
            wait_rows(scattered, nv_ref[jnp.maximum(i - 1, 0)], 1 - slot)

        wait_rows(scattered, nv, slot)


def moe_sparse(x, grp, g, wr, br, wg, wu, wd, tm):
    T, D = x.shape
    Hd = wg.shape[-1]
    tile_grp, n_valid, tok = _dispatch_plan(grp, tm)
    n_tiles = tok.shape[0]
    const = lambda shape: pl.BlockSpec(shape, lambda i, tg, nv: (0,) * len(shape))
    by_group = lambda shape: pl.BlockSpec((None,) + shape, lambda i, tg, nv: (tg[i], 0, 0, 0))
    return pl.pallas_call(
        _moe_body,
        grid_spec=pltpu.PrefetchScalarGridSpec(
            num_scalar_prefetch=2,
            grid=(n_tiles,),
            in_specs=[pl.BlockSpec((None, 1, tm), lambda i, tg, nv: (i, 0, 0), memory_space=pltpu.SMEM),
                      pl.BlockSpec((None, 1, tm), lambda i, tg, nv: (jnp.minimum(i + 1, n_tiles - 1), 0, 0),
                                   memory_space=pltpu.SMEM),
                      pl.BlockSpec(memory_space=pl.ANY),
                      const((1, D)), const((D, LANES)), const((1, LANES)),
                      by_group((EXPERTS_PER_GROUP, D, Hd)), by_group((EXPERTS_PER_GROUP, D, Hd)),
                      by_group((EXPERTS_PER_GROUP, Hd, D))],
            out_specs=pl.BlockSpec(memory_space=pl.ANY),
            scratch_shapes=[pltpu.VMEM((2, tm, D), F32), pltpu.VMEM((2, tm, D), F32),
                            pltpu.SemaphoreType.DMA((2,)), pltpu.SemaphoreType.DMA((2,))]),
        out_shape=jax.ShapeDtypeStruct((T, D), F32),
        compiler_params=_cparams(("arbitrary",), disable_bounds_checks=True),
    )(tile_grp, n_valid, tok, tok, x, g.reshape(1, D), wr, br, wg, wu, wd)


def _final_norm_body(x_ref, g_ref, o_ref):
    o_ref[...] = _rms(x_ref[...], g_ref[...])


def final_norm(x, g, tm):
    T, D = x.shape
    return pl.pallas_call(
        _final_norm_body,
        grid=(T // tm,),
        in_specs=[pl.BlockSpec((tm, D), lambda i: (i, 0)), pl.BlockSpec((1, D), lambda i: (0, 0))],
        out_specs=pl.BlockSpec((tm, D), lambda i: (i, 0)),
        out_shape=jax.ShapeDtypeStruct((T, D), F32),
        compiler_params=_cparams(("parallel",)),
    )(x, g.reshape(1, D))


_C_PU = GLA_PACK
_C_DQ = _C_PU + POOL_WIDTH
_C_DK = _C_DQ + DIFF_QK
_C_DV = _C_DK + DIFF_QK
_C_END = _C_DV + DIFF_V
_IN_PLAN = (('rows', 0, GLA_PACK, F32), ('rows', _C_PU, _C_DQ, F32),
            ('heads', _C_DK, _C_DV, F32), ('heads', _C_DV, _C_END, F32),
            ('rows', _C_DQ, _C_DK, BF16), ('rows', _C_DK, _C_DV, BF16), ('rows', _C_DV, _C_END, BF16))


def _row_tile(T):
    return min(512, T)


def _layer(p, layer, lam_init, x, gla_s0, pool_hist, kv_past, kv_new, mk, mv):
    B, L, D = x.shape
    T = B * L
    tm = _row_tile(T)
    lam_vecs = (p['lambda_q1'], p['lambda_k1'], p['lambda_q2'], p['lambda_k2'])
    zg, pu, dk, dv, dq, dk16, dv16 = norm_proj(x.reshape(T, D), p['norm_mix_g'], p['w_in'], _IN_PLAN, tm,
                                               layer, kv_new)
    q3, k3, v3 = (t.reshape(B, L, DIFF_QK) for t in (dq, dk16, dv16))
    if kv_past is None:
        P = 0
        o_diff = diff_attention_prompt(q3, k3, v3, lam_vecs, p['diff_norm_g'], lam_init)
    else:
        P = kv_past[0].shape[2] // DIFF_HEADS
        o_diff = diff_attention_sample(q3, kv_past[0], kv_past[1], layer, k3, v3, lam_vecs, p['diff_norm_g'], lam_init)
    o_gla, gla_s = gla_mixer(zg.reshape(B, L, GLA_PACK), p['w_gla_a2'], p['b_gla_a'], p['gla_norm_g'], gla_s0)
    o_pool, pool_buf = pool_mixer(pu.reshape(B, L, POOL_WIDTH), pool_hist, P, p['pool_w'], p['pool_scale'])
    M = mk.shape[1]
    x3, grp = mix_mem(x, o_gla, o_pool, o_diff, p['w_out'], p['norm_mem_g'], p['w_mem_q'],
                      mk.reshape(B, M, MEM_HEADS * MEM_DH), mv.reshape(B, M, MEM_HEADS * MEM_DH), p['w_mem_o'],
                      p['norm_ffn_g'], p['w_router_group'], p['b_router_group'])
    x4 = moe_sparse(x3.reshape(T, D), grp, p['norm_ffn_g'], p['w_router'], p['b_router'],
                    p['w_exp_gate'], p['w_exp_up'], p['w_exp_down'], min(tm, T // 8))
    return x4.reshape(B, L, D), gla_s, pool_buf, (dk, dv)


def kernel(x_prompt, x_sample, mem_prompt, cache_diff_k, cache_diff_v, cache_mem_k, cache_mem_v, state_gla, state_pool, norm_mix_g, w_in, w_gla_a2, b_gla_a, gla_norm_g, pool_w, pool_scale, lambda_q1, lambda_k1, lambda_q2, lambda_k2, diff_norm_g, w_out, norm_mem_g, mem_norm_g, w_mem_q, w_mem_kv, w_mem_o, norm_ffn_g, w_router_group, b_router_group, w_router_expert, b_router_expert, w_exp_gate, w_exp_up, w_exp_down, final_norm_g):
    depth = w_in.shape[0]
    D = x_prompt.shape[-1]
    bp, mem_len = mem_prompt.shape[0], mem_prompt.shape[1]
    xp, xs = x_prompt, x_sample
    n_gla_in = 2 * GLA_QK + 2 * GLA_V + GLA_GATE_RANK
    mem_hd = MEM_HEADS * MEM_DH
    grouped = (N_GROUPS, EXPERTS_PER_GROUP)
    outs = [[] for _ in range(6)]
    bs, ls = x_sample.shape[0], x_sample.shape[1]
    lp = x_prompt.shape[1]
    past_len = cache_diff_k.shape[2]
    kv_p = tuple(jnp.zeros((depth, bp * lp * DIFF_HEADS, LANES), F32) for _ in range(2))
    kv_s = tuple(jnp.zeros((depth, bs * ls * DIFF_HEADS, LANES), F32) for _ in range(2))
    kv_cache = tuple(c.reshape(depth, bs, past_len * DIFF_HEADS, LANES) for c in (cache_diff_k, cache_diff_v))
    for l in range(depth):
        wi = w_in[l]
        w_in_p = jnp.concatenate([wi[:, :n_gla_in], jnp.zeros((D, GLA_PACK - n_gla_in), wi.dtype), wi[:, n_gla_in:]],
                                 axis=1).astype(BF16)
        pad = LANES - N_GROUPS - N_EXPERTS
        w_router = jnp.concatenate([w_router_group[l], w_router_expert[l], jnp.zeros((D, pad), F32)], axis=1).astype(BF16)
        b_router = jnp.concatenate([b_router_group[l], b_router_expert[l], jnp.zeros((pad,), F32)]).reshape(1, LANES)
        p = dict(norm_mix_g=norm_mix_g[l], w_in=w_in_p, w_gla_a2=w_gla_a2[l], b_gla_a=b_gla_a[l],
                 gla_norm_g=gla_norm_g[l], pool_w=pool_w[l], pool_scale=pool_scale[l],
                 lambda_q1=lambda_q1[l], lambda_k1=lambda_k1[l], lambda_q2=lambda_q2[l], lambda_k2=lambda_k2[l],
                 diff_norm_g=diff_norm_g[l], w_out=w_out[l].astype(BF16), norm_mem_g=norm_mem_g[l],
                 w_mem_q=w_mem_q[l].astype(BF16), w_mem_o=w_mem_o[l].astype(BF16), norm_ffn_g=norm_ffn_g[l],
                 w_router=w_router, b_router=b_router,
                 w_router_group=w_router_group[l], b_router_group=b_router_group[l],
                 w_exp_gate=w_exp_gate[l].astype(BF16).reshape(grouped + w_exp_gate.shape[2:]),
                 w_exp_up=w_exp_up[l].astype(BF16).reshape(grouped + w_exp_up.shape[2:]),
                 w_exp_down=w_exp_down[l].astype(BF16).reshape(grouped + w_exp_down.shape[2:]))
        lam_init = 0.8 - 0.6 * math.exp(-0.3 * l)
        mk, mv = norm_proj(mem_prompt.reshape(bp * mem_len, D), mem_norm_g[l], w_mem_kv[l].astype(BF16),
                           (('rows', 0, mem_hd, F32), ('rows', mem_hd, 2 * mem_hd, F32)), _row_tile(bp * mem_len))
        mk = mk.reshape(bp, mem_len, MEM_HEADS, MEM_DH)
        mv = mv.reshape(bp, mem_len, MEM_HEADS, MEM_DH)
        xp, g_new, pool_new, kv_p = _layer(
            p, l, lam_init, xp, jnp.zeros((bp, GLA_HEADS, GLA_DK, GLA_DV), F32),
            jnp.zeros((bp, POOL_HIST, POOL_WIDTH), F32), None, kv_p, mk, mv)
        for lst, t in zip(outs[:4], (mk, mv, g_new, pool_new)):
            lst.append(t)
        xs, g_new, pool_new, kv_s = _layer(
            p, l, lam_init, xs, state_gla[l], state_pool[l], kv_cache, kv_s, cache_mem_k[l], cache_mem_v[l])
        for lst, t in zip(outs[4:], (g_new, pool_new)):
            lst.append(t)
    y_prompt = final_norm(xp.reshape(-1, D), final_norm_g, _row_tile(xp.shape[0] * xp.shape[1])).reshape(xp.shape)
    y_sample = final_norm(xs.reshape(-1, D), final_norm_g, _row_tile(xs.shape[0] * xs.shape[1])).reshape(xs.shape)
    mk_p, mv_p, gs_p, ps_p, gs_s, ps_s = (jnp.stack(o) for o in outs)
    kv_shape_p = (depth, bp, lp, DIFF_HEADS, LANES)
    kv_shape_s = (depth, bs, ls, DIFF_HEADS, LANES)
    return (y_prompt, y_sample, kv_p[0].reshape(kv_shape_p), kv_p[1].reshape(kv_shape_p), mk_p, mv_p, gs_p, ps_p,
            kv_s[0].reshape(kv_shape_s), kv_s[1].reshape(kv_shape_s), gs_s, ps_s)
```
